```python
import math
import jax
import jax.numpy as jnp
from jax import lax
import numpy as np

D_MODEL = 1024
BATCH = 8
SEQ = 2048
DEPTH = 4
DEC_BATCH = 128
DEC_SEQ = 8
PAST_LEN = 2048
PAGE_SIZE = 128

N_EVEN = (DEPTH + 1) // 2
N_ODD = DEPTH // 2
MIX_HALF = D_MODEL // 2
HG_HEADS = 4
HG_DK = MIX_HALF // HG_HEADS
HG_DV = MIX_HALF // HG_HEADS
HG_CHUNK = 64
SB_HEADS = 8
SB_HD = MIX_HALF // SB_HEADS
SB_QBLOCK = 128
SB_BIAS_INIT = -7.0
POOL_WINDOWS = (2, 4, 8, 16)
POOL_GROUPS = len(POOL_WINDOWS)
POOL_GC = MIX_HALF // POOL_GROUPS
POOL_HIST = max(POOL_WINDOWS) - 1
CONV_CH = MIX_HALF
CONV_WIDTH = 31
CONV_HIST = CONV_WIDTH - 1
D_FF = 4 * D_MODEL
HG_QK = HG_HEADS * HG_DK
HG_V = HG_HEADS * HG_DV
SB_W = SB_HEADS * SB_HD
EVEN_IN = 2 * HG_QK + 2 * HG_V + 3 * SB_W
ODD_IN = MIX_HALF + 2 * CONV_CH
EPS = 1e-6
F32 = jnp.float32

kernel_name = 'hybrid_hgrn2_stickbreak_pool_conformer_step'


def rms_norm(x, g):
    xf = x.astype(F32)
    y = xf * lax.rsqrt(jnp.mean(xf * xf, axis=-1, keepdims=True) + EPS)
    return (y * g.astype(F32)).astype(x.dtype)


def mlp(h, w1, w2):
    return jnp.square(jax.nn.relu(h @ w1)) @ w2


def stick_breaking(q, k, v, bias, q_start):
    B, Tq, H, d = q.shape
    Tk = k.shape[1]
    blk = math.gcd(Tq, SB_QBLOCK)
    nb = Tq // blk
    qb = q.reshape(B, nb, blk, H, d).transpose(1, 0, 2, 3, 4)
    k_pos = jnp.arange(Tk)
    kf = k.astype(F32)
    vf = v.astype(F32)
    scale = d ** -0.5
    bf = bias.astype(F32)[None, :, None, None]

    def one_block(args):
        qi, bi = args
        q_pos = q_start + bi * blk + jnp.arange(blk)
        z = jnp.einsum('bqhd,bkhd->bhqk', qi.astype(F32), kf) * scale + bf
        mask = k_pos[None, :] < q_pos[:, None]
        log_beta = jax.nn.log_sigmoid(z)
        log_keep = jnp.where(mask, jax.nn.log_sigmoid(-z), 0.0)
        later = lax.cumsum(log_keep, axis=3, reverse=True) - log_keep
        w = jnp.where(mask, jnp.exp(log_beta + later), 0.0)
        return jnp.einsum('bhqk,bkhd->bqhd', w, vf)

    o = lax.map(one_block, (qb, jnp.arange(nb)))
    return o.transpose(1, 0, 2, 3, 4).reshape(B, Tq, H, d).astype(q.dtype)


def hgrn2(q, f_logit, val, lb, S0):
    B, T, H, DK = q.shape
    DV = val.shape[-1]
    C = math.gcd(T, HG_CHUNK)
    n = T // C
    lbf = lb.astype(F32)
    fl = f_logit.astype(F32)
    log_f = jnp.logaddexp(jnp.log(lbf), jnp.log1p(-lbf) + jax.nn.log_sigmoid(fl))
    k = (1.0 - lbf) * jax.nn.sigmoid(-fl)

    def to_chunks(a):
        return a.astype(F32).reshape(B, n, C, H, a.shape[-1]).transpose(1, 0, 3, 2, 4)

    causal = jnp.tril(jnp.ones((C, C), bool))

    def step(S, inp):
        qc, kc, lfc, vc = inp
        b = jnp.cumsum(lfc, axis=2)
        o_inter = jnp.einsum('bhtk,bhkv->bhtv', qc * jnp.exp(b), S)
        diff = jnp.where(causal[:, :, None], b[:, :, :, None, :] - b[:, :, None, :, :], -jnp.inf)
        a = jnp.einsum('bhtk,bhtsk,bhsk->bhts', qc, jnp.exp(diff), kc)
        o = o_inter + jnp.einsum('bhts,bhsv->bhtv', a, vc)
        b_last = b[:, :, -1:, :]
        S_new = jnp.exp(b_last[:, :, 0, :])[..., None] * S + jnp.einsum(
            'bhsk,bhsv->bhkv', kc * jnp.exp(b_last - b), vc)
        return S_new, o

    S_fin, o = lax.scan(step, S0.astype(F32),
                        (to_chunks(q), to_chunks(k), to_chunks(log_f), to_chunks(val)))
    o = o.transpose(1, 0, 3, 2, 4).reshape(B, T, H, DV)
    return o, S_fin


def even_mixer(h, w_in, w_out, sb_bias, lb, hg_g, S0, k_past, v_past, q_start):
    B, T, _ = h.shape
    proj = h @ w_in
    cuts = [HG_QK, 2 * HG_QK, 2 * HG_QK + HG_V, 2 * HG_QK + 2 * HG_V,
            2 * HG_QK + 2 * HG_V + SB_W, 2 * HG_QK + 2 * HG_V + 2 * SB_W]
    q_a, f_a, i_a, g_a, q_b, k_b, v_b = jnp.split(proj, cuts, axis=-1)
    o_a, S_new = hgrn2(q_a.reshape(B, T, HG_HEADS, HG_DK), f_a.reshape(B, T, HG_HEADS, HG_DK),
                       i_a.reshape(B, T, HG_HEADS, HG_DV), lb, S0)
    o_a = o_a * lax.rsqrt(jnp.mean(o_a * o_a, axis=-1, keepdims=True) + EPS) * hg_g.astype(F32)
    o_a = o_a.reshape(B, T, HG_V) * jax.nn.silu(g_a.astype(F32))
    q_b = q_b.reshape(B, T, SB_HEADS, SB_HD)
    k_b = k_b.reshape(B, T, SB_HEADS, SB_HD)
    v_b = v_b.reshape(B, T, SB_HEADS, SB_HD)
    if k_past is None:
        k_all, v_all = k_b, v_b
    else:
        k_all = jnp.concatenate([k_past.astype(k_b.dtype), k_b], axis=1)
        v_all = jnp.concatenate([v_past.astype(v_b.dtype), v_b], axis=1)
    o_b = stick_breaking(q_b, k_all, v_all, sb_bias, q_start).reshape(B, T, SB_W)
    out = jnp.concatenate([o_a.astype(h.dtype), o_b.astype(h.dtype)], axis=-1) @ w_out
    return out, S_new, k_b, v_b


def odd_mixer(h, w_in, w_out, pool_w, pool_scale, conv_w, conv_b, ln_g, ln_b,
              pool_hist, conv_hist, q_start):
    B, T, _ = h.shape
    proj = h @ w_in
    u, a, gt = jnp.split(proj, [MIX_HALF, MIX_HALF + CONV_CH], axis=-1)
    u_ext = jnp.concatenate([pool_hist.astype(u.dtype), u], axis=1)
    cs = jnp.concatenate([jnp.zeros((B, 1, MIX_HALF), F32),
                          jnp.cumsum(u_ext.astype(F32), axis=1)], axis=1)
    pos = q_start + jnp.arange(T)
    end = cs[:, POOL_HIST + 1:POOL_HIST + 1 + T]
    pooled = []
    for gi, w in enumerate(POOL_WINDOWS):
        sl = slice(gi * POOL_GC, (gi + 1) * POOL_GC)
        wsum = end[..., sl] - cs[:, POOL_HIST + 1 - w:POOL_HIST + 1 - w + T, sl]
        cnt = jnp.minimum(w, pos + 1).astype(F32)[None, :, None]
        pooled.append(wsum / cnt)
    pooled = (jnp.concatenate(pooled, axis=-1) - u.astype(F32)).reshape(B, T, POOL_GROUPS, POOL_GC)
    pool_out = jnp.einsum('btgc,gcd->btgd', pooled, pool_w.astype(F32)).reshape(B, T, MIX_HALF)
    pool_out = pool_out * pool_scale.astype(F32)
    glu = a * jax.nn.sigmoid(gt)
    glu_ext = jnp.concatenate([conv_hist.astype(glu.dtype), glu], axis=1)
    cv = lax.conv_general_dilated(glu_ext.astype(F32), conv_w.astype(F32)[:, None, :],
                                  window_strides=(1,), padding='VALID',
                                  dimension_numbers=('NWC', 'WIO', 'NWC'),
                                  feature_group_count=CONV_CH) + conv_b.astype(F32)
    mu = jnp.mean(cv, axis=-1, keepdims=True)
    var = jnp.mean(jnp.square(cv - mu), axis=-1, keepdims=True)
    cv = (cv - mu) * lax.rsqrt(var + EPS) * ln_g.astype(F32) + ln_b.astype(F32)
    conv_out = jax.nn.silu(cv)
    out = jnp.concatenate([pool_out.astype(h.dtype), conv_out.astype(h.dtype)], axis=-1) @ w_out
    return out, u_ext[:, -POOL_HIST:], glu_ext[:, -CONV_HIST:]


def setup_inputs(seed: int = 0) -> dict:
    key = jax.random.key(seed)
    ks = jax.random.split(key, 32)
    n_pages = PAST_LEN // PAGE_SIZE
    n_pool = (DEC_BATCH * n_pages * 5) // 4

    def nrm(k, shape, s):
        return jax.random.normal(k, shape, F32) * s

    page_table = jax.random.permutation(ks[9], n_pool)[:DEC_BATCH * n_pages]
    page_table = page_table.reshape(DEC_BATCH, n_pages).astype(jnp.int32)
    return {
        'x_prompt': nrm(ks[0], (BATCH, SEQ, D_MODEL), 1.0),
        'x_sample': nrm(ks[1], (DEC_BATCH, DEC_SEQ, D_MODEL), 1.0),
        'c_prompt': nrm(ks[2], (BATCH, D_MODEL), 1.0),
        'c_sample': nrm(ks[3], (DEC_BATCH, D_MODEL), 1.0),
        'cache_sb_k': nrm(ks[4], (N_EVEN, n_pool, PAGE_SIZE, SB_HEADS, SB_HD), 1.0),
        'cache_sb_v': nrm(ks[5], (N_EVEN, n_pool, PAGE_SIZE, SB_HEADS, SB_HD), 1.0),
        'state_hgrn': nrm(ks[6], (N_EVEN, DEC_BATCH, HG_HEADS, HG_DK, HG_DV), 0.5),
        'state_pool': nrm(ks[7], (N_ODD, DEC_BATCH, POOL_HIST, MIX_HALF), 1.0),
        'state_conv': nrm(ks[8], (N_ODD, DEC_BATCH, CONV_HIST, CONV_CH), 0.5),
        'page_table': page_table,
        'norm_g': 1.0 + nrm(ks[10], (DEPTH, 4, D_MODEL), 0.05),
        'ada_w': nrm(ks[11], (DEPTH, D_MODEL, 6 * D_MODEL), D_MODEL ** -0.5),
        'ada_b': nrm(ks[12], (DEPTH, 6 * D_MODEL), 0.02),
        'even_w_in': nrm(ks[13], (N_EVEN, D_MODEL, EVEN_IN), D_MODEL ** -0.5),
        'even_w_out': nrm(ks[14], (N_EVEN, HG_V + SB_W, D_MODEL), (HG_V + SB_W) ** -0.5),
        'sb_bias': SB_BIAS_INIT + nrm(ks[27], (N_EVEN, SB_HEADS), 0.1),
        'hgrn_lb_logits': nrm(ks[15], (N_EVEN, HG_QK), 0.5),
        'hgrn_norm_g': 1.0 + nrm(ks[16], (N_EVEN, HG_HEADS, HG_DV), 0.05),
        'odd_w_in': nrm(ks[17], (N_ODD, D_MODEL, ODD_IN), D_MODEL ** -0.5),
        'odd_w_out': nrm(ks[18], (N_ODD, MIX_HALF + CONV_CH, D_MODEL), (MIX_HALF + CONV_CH) ** -0.5),
        'pool_w': nrm(ks[19], (N_ODD, POOL_GROUPS, POOL_GC, POOL_GC), POOL_GC ** -0.5),
        'pool_scale': 1.0 + nrm(ks[20], (N_ODD, MIX_HALF), 0.05),
        'conv_w': nrm(ks[21], (N_ODD, CONV_WIDTH, CONV_CH), CONV_WIDTH ** -0.5),
        'conv_b': nrm(ks[22], (N_ODD, CONV_CH), 0.02),
        'conv_ln_g': 1.0 + nrm(ks[23], (N_ODD, CONV_CH), 0.05),
        'conv_ln_b': nrm(ks[24], (N_ODD, CONV_CH), 0.02),
        'mlp_w1': nrm(ks[25], (DEPTH, D_MODEL, D_FF), D_MODEL ** -0.5),
        'mlp_w2': nrm(ks[26], (DEPTH, D_FF, D_MODEL), D_FF ** -0.5),
    }


def reference(x_prompt, x_sample, c_prompt, c_sample, cache_sb_k, cache_sb_v, state_hgrn,
              state_pool, state_conv, page_table, norm_g, ada_w, ada_b, even_w_in, even_w_out,
              sb_bias, hgrn_lb_logits, hgrn_norm_g, odd_w_in, odd_w_out, pool_w, pool_scale, conv_w,
              conv_b, conv_ln_g, conv_ln_b, mlp_w1, mlp_w2):
    lb_cum = jnp.cumsum(jax.nn.softmax(hgrn_lb_logits.astype(F32), axis=0), axis=0)
    lower_bounds = jnp.maximum(lb_cum - lb_cum[:1], 0.0).reshape(N_EVEN, HG_HEADS, HG_DK)
    n_pages = page_table.shape[1]
    past_len = n_pages * cache_sb_k.shape[2]

    def run(x, c, q_start, is_sample):
        B = x.shape[0]
        S_list, k_list, v_list, p_list, cv_list = [], [], [], [], []
        for l in range(DEPTH):
            mod = (jax.nn.silu(c) @ ada_w[l] + ada_b[l])[:, None, :]
            sh1, sc1, g1, sh2, sc2, g2 = jnp.split(mod, 6, axis=-1)
            h = rms_norm(x, norm_g[l, 0]) * (1 + sc1) + sh1
            if l % 2 == 0:
                ie = l // 2
                if is_sample:
                    S0 = state_hgrn[ie]
                    k_past = cache_sb_k[ie][page_table].reshape(B, past_len, SB_HEADS, SB_HD)
                    v_past = cache_sb_v[ie][page_table].reshape(B, past_len, SB_HEADS, SB_HD)
                else:
                    S0 = jnp.zeros((B, HG_HEADS, HG_DK, HG_DV), F32)
                    k_past = None
                    v_past = None
                m, S_new, k_new, v_new = even_mixer(h, even_w_in[ie], even_w_out[ie], sb_bias[ie],
                                                    lower_bounds[ie], hgrn_norm_g[ie], S0, k_past,
                                                    v_past, q_start)
                S_list.append(S_new)
                k_list.append(k_new)
                v_list.append(v_new)
            else:
                io = l // 2
                if is_sample:
                    ph = state_pool[io]
                    chh = state_conv[io]
                else:
                    ph = jnp.zeros((B, POOL_HIST, MIX_HALF), x.dtype)
                    chh = jnp.zeros((B, CONV_HIST, CONV_CH), x.dtype)
                m, p_new, c_new = odd_mixer(h, odd_w_in[io], odd_w_out[io], pool_w[io], pool_scale[io],
                                            conv_w[io], conv_b[io], conv_ln_g[io], conv_ln_b[io],
                                            ph, chh, q_start)
                p_list.append(p_new)
                cv_list.append(c_new)
            x = x + g1 * rms_norm(m, norm_g[l, 1])
            h = rms_norm(x, norm_g[l, 2]) * (1 + sc2) + sh2
            x = x + g2 * rms_norm(mlp(h, mlp_w1[l], mlp_w2[l]), norm_g[l, 3])
        return (x, jnp.stack(S_list), jnp.stack(k_list), jnp.stack(v_list),
                jnp.stack(p_list), jnp.stack(cv_list))

    y_p, S_p, k_p, v_p, pool_p, conv_p = run(x_prompt, c_prompt, 0, False)
    y_s, S_s, k_s, v_s, pool_s, conv_s = run(x_sample, c_sample, past_len, True)
    return (y_p, y_s, k_p, v_p, k_s, v_s, S_p, S_s, pool_p, pool_s, conv_p, conv_s)
```

```python
import functools
import math

import jax
import jax.numpy as jnp
from jax import lax
from jax.experimental import pallas as pl
from jax.experimental.pallas import tpu as pltpu

F32 = jnp.float32
BF16 = jnp.bfloat16

D_MODEL = 1024
DEPTH = 4
MIX_HALF = D_MODEL // 2
HG_HEADS = 4
HG_D = MIX_HALF // HG_HEADS
SB_HEADS = 8
SB_HD = MIX_HALF // SB_HEADS
POOL_WINDOWS = (2, 4, 8, 16)
POOL_GC = MIX_HALF // len(POOL_WINDOWS)
POOL_HIST = max(POOL_WINDOWS) - 1
CONV_WIDTH = 31
CONV_HIST = CONV_WIDTH - 1
D_FF = 4 * D_MODEL
EVEN_IN = 7 * MIX_HALF
ODD_IN = 3 * MIX_HALF
EPS = 1e-6

LANES = 128
SUBLANES = 8
VMEM_LIMIT = 52 * 1024 * 1024

ROW_TILE = 512
FF_TILE = 1024
HG_CHUNK = 128
HG_SUB = 16
SB_BLOCK = 256


def _cparams(*sem):
    return pltpu.CompilerParams(dimension_semantics=sem, vmem_limit_bytes=VMEM_LIMIT)


def _dot(a, b):
    return jnp.dot(a, b, preferred_element_type=F32)


def _dot_nt(a, b):
    return lax.dot_general(a, b, (((1,), (1,)), ((), ())), preferred_element_type=F32)


def _split_dot(x, m_bf16, terms):
    acc = None
    r = x
    for i in range(terms):
        p = r.astype(BF16)
        d = _dot(p, m_bf16)
        acc = d if acc is None else acc + d
        if i + 1 < terms:
            r = r - p.astype(F32)
    return acc


def _split_dot_left(m_bf16, x, terms):
    acc = None
    r = x
    for i in range(terms):
        p = r.astype(BF16)
        d = _dot(m_bf16, p)
        acc = d if acc is None else acc + d
        if i + 1 < terms:
            r = r - p.astype(F32)
    return acc


def _sigmoid(x):
    return 1.0 / (1.0 + jnp.exp(-x))


def _log_sigmoid(x):
    return jnp.minimum(x, 0.0) - jnp.log1p(jnp.exp(-jnp.abs(x)))


def _rms(x, g):
    return x * lax.rsqrt(jnp.mean(x * x, axis=-1, keepdims=True) + EPS) * g


def _ada_kernel(c_ref, w_ref, b_ref, o_ref):
    c = c_ref[...]
    s = (c * _sigmoid(c)).astype(BF16)
    o_ref[0] = _dot(s, w_ref[0].astype(BF16)) + b_ref[0]


def _ada(c_all, ada_w, ada_b):
    nb = c_all.shape[0]
    tn = 1536
    return pl.pallas_call(
        _ada_kernel,
        grid=(DEPTH, 6 * D_MODEL // tn),
        in_specs=[
            pl.BlockSpec((nb, D_MODEL), lambda l, j: (0, 0)),
            pl.BlockSpec((1, D_MODEL, tn), lambda l, j: (l, 0, j)),
            pl.BlockSpec((1, 1, tn), lambda l, j: (l, 0, j)),
        ],
        out_specs=pl.BlockSpec((1, nb, tn), lambda l, j: (l, 0, j)),
        out_shape=jax.ShapeDtypeStruct((DEPTH, nb, 6 * D_MODEL), F32),
        compiler_params=_cparams("arbitrary", "arbitrary"),
        name="ada",
    )(c_all, ada_w, ada_b.reshape(DEPTH, 1, 6 * D_MODEL))


def _mod_spec(mod, tm, chunk):
    if mod.shape[1] == 1:
        return pl.BlockSpec((1, 1, D_MODEL), lambda b, t, *_: (b, 0, chunk))
    return pl.BlockSpec((1, tm, D_MODEL), lambda b, t, *_: (b, t, chunk))


def _inproj_kernel(x_ref, sh_ref, sc_ref, g_ref, w_ref, *o_refs):
    h = _rms(x_ref[0], g_ref[0:1, :]) * (1.0 + sc_ref[0]) + sh_ref[0]
    y = _dot(h.astype(BF16), w_ref[...])
    c0 = 0
    for o_ref in o_refs:
        n = o_ref.shape[-1]
        o_ref[0] = y[:, c0:c0 + n]
        c0 += n


def _in_proj(x, mod, norm_g_l, w_bf16, widths):
    bx, tx, _ = x.shape
    tm = min(tx, ROW_TILE)
    n = w_bf16.shape[1]
    return pl.pallas_call(
        _inproj_kernel,
        grid=(bx, tx // tm),
        in_specs=[
            pl.BlockSpec((1, tm, D_MODEL), lambda b, t: (b, t, 0)),
            _mod_spec(mod, tm, 0),
            _mod_spec(mod, tm, 1),
            pl.BlockSpec((4, D_MODEL), lambda b, t: (0, 0)),
            pl.BlockSpec((D_MODEL, n), lambda b, t: (0, 0)),
        ],
        out_specs=[pl.BlockSpec((1, tm, wd), lambda b, t: (b, t, 0)) for wd in widths],
        out_shape=[jax.ShapeDtypeStruct((bx, tx, wd), F32) for wd in widths],
        compiler_params=_cparams("arbitrary", "arbitrary"),
        name="in_proj",
    )(x, mod, mod, norm_g_l, w_bf16)


def _hgrn_kernel(q_ref, f_ref, i_ref, g_ref, lb_ref, gn_ref, s0_ref, o_ref, sout_ref,
                 st_scr, b_scr, kk_scr, oi_scr, *, chunk, sub):
    t = pl.program_id(2)

    @pl.when(t == 0)
    def _():
        st_scr[...] = s0_ref[0, 0].T

    lb = lb_ref[...]
    q = q_ref[0]
    fl = f_ref[0]
    v = i_ref[0]
    c = jnp.log1p(-lb) + _log_sigmoid(fl)
    a = jnp.log(lb)
    log_f = jnp.maximum(a, c) + jnp.log1p(jnp.exp(-jnp.abs(a - c)))
    kk = (1.0 - lb) * _sigmoid(-fl)
    ri = lax.broadcasted_iota(jnp.int32, (chunk, chunk), 0)
    ci = lax.broadcasted_iota(jnp.int32, (chunk, chunk), 1)
    tril = jnp.where(ri >= ci, 1.0, 0.0).astype(BF16)
    b = _split_dot_left(tril, log_f, 3)
    b_scr[...] = b
    kk_scr[...] = kk
    st = st_scr[...]
    oi_scr[...] = _dot_nt((q * jnp.exp(b)).astype(BF16), st.astype(BF16))
    b_last = b[chunk - 1:chunk, :]
    kd = kk * jnp.exp(b_last - b)
    vp = v
    if chunk < HG_D:
        pad = jnp.zeros((HG_D - chunk, HG_D), F32)
        kd = jnp.concatenate([kd, pad], axis=0)
        vp = jnp.concatenate([v, pad], axis=0)
    st_scr[...] = jnp.exp(b_last) * st + _dot(vp.T.astype(BF16), kd.astype(BF16))

    rows = lax.broadcasted_iota(jnp.int32, (chunk, 1), 0)
    srow = lax.broadcasted_iota(jnp.int32, (sub, 1), 0)
    gn = gn_ref[...]

    def sub_block(i, carry):
        r0 = pl.multiple_of(i * sub, sub)
        b_i = b_scr[pl.ds(r0, sub), :]
        q_i = q_ref[0, pl.ds(r0, sub), :]
        kk_i = kk_scr[pl.ds(r0, sub), :]
        v_i = i_ref[0, pl.ds(r0, sub), :]
        o = oi_scr[pl.ds(r0, sub), :]
        if chunk > sub:
            anchor = b_i[0:1, :]
            qt = q_i * jnp.exp(b_i - anchor)
            kt = jnp.where(rows < r0, kk_scr[...] * jnp.exp(jnp.minimum(anchor - b_scr[...], 0.0)), 0.0)
            a_off = _dot_nt(qt.astype(BF16), kt.astype(BF16))
            o = o + _dot(a_off.astype(BF16), i_ref[0].astype(BF16))
        o_diag = jnp.zeros((sub, HG_D), F32)
        for tt in range(sub):
            e = jnp.exp(jnp.minimum(b_i[tt:tt + 1, :] - b_i, 0.0))
            d = jnp.sum(q_i[tt:tt + 1, :] * e * kk_i, axis=-1, keepdims=True)
            d = jnp.where(srow <= tt, d, 0.0)
            row = jnp.sum(d * v_i, axis=0, keepdims=True)
            o_diag = jnp.where(srow == tt, row, o_diag)
        o = o + o_diag
        g_i = g_ref[0, pl.ds(r0, sub), :]
        o = _rms(o, gn) * (g_i * _sigmoid(g_i))
        o_ref[0, pl.ds(r0, sub), :] = o
        return carry

    lax.fori_loop(0, chunk // sub, sub_block, 0)

    @pl.when(t == pl.num_programs(2) - 1)
    def _():
        sout_ref[0, 0] = st_scr[...].T


def _hgrn(qfig, lb_l, gn_l, s0):
    bx, tx, _ = qfig.shape
    chunk = min(tx, HG_CHUNK)
    sub = min(chunk, HG_SUB)

    def col(k):
        return pl.BlockSpec((1, chunk, HG_D), lambda b, h, t: (b, t, k * HG_HEADS + h))

    head_vec = pl.BlockSpec((1, HG_D), lambda b, h, t: (0, h))
    state = pl.BlockSpec((1, 1, HG_D, HG_D), lambda b, h, t: (b, h, 0, 0))
    return pl.pallas_call(
        functools.partial(_hgrn_kernel, chunk=chunk, sub=sub),
        grid=(bx, HG_HEADS, tx // chunk),
        in_specs=[col(0), col(1), col(2), col(3), head_vec, head_vec, state],
        out_specs=[pl.BlockSpec((1, chunk, HG_D), lambda b, h, t: (b, t, h)), state],
        out_shape=[jax.ShapeDtypeStruct((bx, tx, MIX_HALF), F32),
                   jax.ShapeDtypeStruct((bx, HG_HEADS, HG_D, HG_D), F32)],
        scratch_shapes=[pltpu.VMEM((HG_D, HG_D), F32), pltpu.VMEM((chunk, HG_D), F32),
                        pltpu.VMEM((chunk, HG_D), F32), pltpu.VMEM((chunk, HG_D), F32)],
        compiler_params=_cparams("arbitrary", "arbitrary", "arbitrary"),
        name="hgrn",
    )(qfig, qfig, qfig, qfig, lb_l, gn_l, s0)


def _sb_block(qh, kblk, vblk, bias, carry, acc, upper, mask, transposed=False):
    z = (_dot(qh, kblk) if transposed else _dot_nt(qh, kblk)) * (SB_HD ** -0.5) + bias
    ls = _log_sigmoid(z)
    lk = ls - z
    if mask is not None:
        lk = jnp.where(mask, lk, 0.0)
    later = _split_dot(lk, upper, 2) + carry
    w = jnp.exp(ls + later)
    if mask is not None:
        w = jnp.where(mask, w, 0.0)
    wb = w.astype(BF16)
    acc = acc + (_dot_nt(wb, vblk) if transposed else _dot(wb, vblk))
    carry = carry + jnp.sum(lk, axis=-1, keepdims=True)
    return carry, acc


def _sb_prompt_kernel(bias_ref, q_ref, k_ref, v_ref, o_ref, *, blk):
    hp = pl.program_id(1)
    qi = pl.program_id(2)
    lane = lax.broadcasted_iota(jnp.int32, (1, LANES), 1)
    ri = lax.broadcasted_iota(jnp.int32, (blk, blk), 0)
    ci = lax.broadcasted_iota(jnp.int32, (blk, blk), 1)
    upper = jnp.where(ri > ci, 1.0, 0.0).astype(BF16)
    causal = ci < ri
    q = q_ref[0]
    outs = []
    for hh in range(2):
        in_head = (lane >= hh * SB_HD) & (lane < (hh + 1) * SB_HD)
        qh = jnp.where(in_head, q, 0.0).astype(BF16)
        bias = bias_ref[2 * hp + hh]

        def kv(kb):
            r0 = pl.multiple_of(kb * blk, blk)
            return k_ref[0, pl.ds(r0, blk), :].astype(BF16), v_ref[0, pl.ds(r0, blk), :].astype(BF16)

        kblk, vblk = kv(qi)
        carry, acc = _sb_block(qh, kblk, vblk, bias, jnp.zeros((blk, 1), F32), jnp.zeros((blk, LANES), F32),
                               upper, causal)

        def body(j, ca):
            kblk, vblk = kv(qi - 1 - j)
            return _sb_block(qh, kblk, vblk, bias, ca[0], ca[1], upper, None)

        carry, acc = lax.fori_loop(0, qi, body, (carry, acc))
        outs.append(acc)
    o_ref[0] = jnp.where(lane < SB_HD, outs[0], outs[1])


def _sb_prompt(bias_l, q, k, v):
    bx, tx, _ = q.shape
    blk = SB_BLOCK
    pair = pl.BlockSpec((1, blk, LANES), lambda b, hp, qi: (b, qi, hp))
    full = pl.BlockSpec((1, tx, LANES), lambda b, hp, qi: (b, 0, hp))
    return pl.pallas_call(
        functools.partial(_sb_prompt_kernel, blk=blk),
        grid=(bx, SB_HEADS // 2, tx // blk),
        in_specs=[pl.BlockSpec(memory_space=pltpu.SMEM), pair, full, full],
        out_specs=pair,
        out_shape=jax.ShapeDtypeStruct((bx, tx, MIX_HALF), F32),
        compiler_params=_cparams("arbitrary", "arbitrary", "arbitrary"),
        name="sb_prompt",
    )(bias_l, q, k, v)


def _sb_sample_kernel(pt_ref, q_ref, kn_ref, vn_ref, bias_ref, kc_ref, vc_ref, o_ref,
                      qbd_scr, acc_scr, carry_scr, *, tq, page):
    p = pl.program_id(1)
    nrow = SB_HEADS * tq
    rowh = lax.broadcasted_iota(jnp.int32, (nrow, MIX_HALF), 0) // tq
    colh = lax.broadcasted_iota(jnp.int32, (nrow, MIX_HALF), 1) // SB_HD
    own = rowh == colh
    bias = bias_ref[...]
    ri = lax.broadcasted_iota(jnp.int32, (page, page), 0)
    ci = lax.broadcasted_iota(jnp.int32, (page, page), 1)
    upper = jnp.where(ri > ci, 1.0, 0.0).astype(BF16)

    @pl.when(p == 0)
    def _():
        q = q_ref[0]
        qbd = jnp.where(own, jnp.concatenate([q] * SB_HEADS, axis=0), 0.0).astype(BF16)
        qbd_scr[...] = qbd
        pad = jnp.zeros((page - tq, MIX_HALF), F32)
        kn = jnp.concatenate([kn_ref[0], pad], axis=0).astype(BF16)
        vn = jnp.concatenate([vn_ref[0], pad], axis=0).astype(BF16)
        qpos = lax.broadcasted_iota(jnp.int32, (nrow, page), 0) % tq
        kpos = lax.broadcasted_iota(jnp.int32, (nrow, page), 1)
        carry, acc = _sb_block(qbd, kn, vn, bias, jnp.zeros((nrow, 1), F32), jnp.zeros((nrow, MIX_HALF), F32),
                               upper, kpos < qpos)
        carry_scr[...] = jnp.broadcast_to(carry, (nrow, LANES))
        acc_scr[...] = acc

    carry, acc = _sb_block(qbd_scr[...], kc_ref[0, 0].astype(BF16), vc_ref[0, 0].astype(BF16), bias,
                           carry_scr[:, 0:1], acc_scr[...], upper, None, transposed=True)
    carry_scr[...] = jnp.broadcast_to(carry, (nrow, LANES))
    acc_scr[...] = acc

    @pl.when(p == pl.num_programs(1) - 1)
    def _():
        masked = jnp.where(own, acc, 0.0)
        out = masked[0:tq, :]
        for h in range(1, SB_HEADS):
            out = out + masked[h * tq:(h + 1) * tq, :]
        o_ref[0] = out


def _sb_sample(layer, page_table, bias_rows, q, k_new, v_new, cache_k, cache_v):
    bx, tq, _ = q.shape
    n_pages = page_table.shape[1]
    page = cache_k.shape[3]
    nrow = SB_HEADS * tq
    seq = pl.BlockSpec((1, tq, MIX_HALF), lambda b, p, pt: (b, 0, 0))
    cache = pl.BlockSpec((1, 1, MIX_HALF, page),
                         lambda b, p, pt: (layer, pt[b * n_pages + (n_pages - 1 - p)], 0, 0))
    return pl.pallas_call(
        functools.partial(_sb_sample_kernel, tq=tq, page=page),
        grid_spec=pltpu.PrefetchScalarGridSpec(
            num_scalar_prefetch=1,
            grid=(bx, n_pages),
            in_specs=[seq, seq, seq, pl.BlockSpec((nrow, 1), lambda b, p, pt: (0, 0)), cache, cache],
            out_specs=seq,
            scratch_shapes=[pltpu.VMEM((nrow, MIX_HALF), BF16), pltpu.VMEM((nrow, MIX_HALF), F32),
                            pltpu.VMEM((nrow, LANES), F32)],
        ),
        out_shape=jax.ShapeDtypeStruct((bx, tq, MIX_HALF), F32),
        compiler_params=_cparams("arbitrary", "arbitrary"),
        name="sb_sample",
    )(page_table.reshape(-1), q, k_new, v_new, bias_rows, cache_k, cache_v)


def _odd_kernel(u_ref, a_ref, gt_ref, hu_ref, hg_ref, pw_ref, ps_ref, cw_ref, cb_ref, lg_ref, lbias_ref,
                po_ref, co_ref, nu_ref, ng_ref, eu_scr, eg_scr, *, tm, q_start):
    t = pl.program_id(1)
    hu0 = 2 * SUBLANES
    hg0 = 4 * SUBLANES

    @pl.when(t == 0)
    def _():
        eu_scr[0:hu0 - POOL_HIST, :] = jnp.zeros((hu0 - POOL_HIST, MIX_HALF), F32)
        eg_scr[0:hg0 - CONV_HIST, :] = jnp.zeros((hg0 - CONV_HIST, MIX_HALF), F32)
        eu_scr[hu0 - POOL_HIST:hu0, :] = hu_ref[0]
        eg_scr[hg0 - CONV_HIST:hg0, :] = hg_ref[0]

    @pl.when(t > 0)
    def _():
        eu_scr[0:hu0, :] = eu_scr[tm:tm + hu0, :]
        eg_scr[0:hg0, :] = eg_scr[tm:tm + hg0, :]

    u = u_ref[0]
    gt = gt_ref[0]
    glu = a_ref[0] * _sigmoid(gt)
    eu_scr[hu0:hu0 + tm, :] = u
    eg_scr[hg0:hg0 + tm, :] = glu

    pos = q_start + t * tm + lax.broadcasted_iota(jnp.int32, (tm, 1), 0)
    for gi, w in enumerate(POOL_WINDOWS):
        c0 = gi * POOL_GC
        wsum = u[:, c0:c0 + POOL_GC]
        for d in range(1, w):
            wsum = wsum + eu_scr[hu0 - d:hu0 - d + tm, c0:c0 + POOL_GC]
        cnt = jnp.minimum(w, pos + 1).astype(F32)
        pooled = wsum / cnt - u[:, c0:c0 + POOL_GC]
        po = _dot(pooled.astype(BF16), pw_ref[gi].astype(BF16))
        po_ref[0, :, c0:c0 + POOL_GC] = po * ps_ref[:, c0:c0 + POOL_GC]

    cv = jnp.broadcast_to(cb_ref[...], (tm, MIX_HALF))
    for j in range(CONV_WIDTH):
        s0 = hg0 - CONV_HIST + j
        cv = cv + cw_ref[j:j + 1, :] * eg_scr[s0:s0 + tm, :]
    mu = jnp.mean(cv, axis=-1, keepdims=True)
    dv = cv - mu
    var = jnp.mean(dv * dv, axis=-1, keepdims=True)
    y = dv * lax.rsqrt(var + EPS) * lg_ref[...] + lbias_ref[...]
    co_ref[0] = y * _sigmoid(y)

    @pl.when(t == pl.num_programs(1) - 1)
    def _():
        nu_ref[0] = eu_scr[hu0 + tm - POOL_HIST:hu0 + tm, :]
        ng_ref[0] = eg_scr[hg0 + tm - CONV_HIST:hg0 + tm, :]


def _odd(u, a, gt, hist_u, hist_g, io, pool_w, pool_scale, conv_w, conv_b, ln_g, ln_b, q_start):
    bx, tx, _ = u.shape
    tm = min(tx, 256)
    tile = pl.BlockSpec((1, tm, MIX_HALF), lambda b, t: (b, t, 0))
    vec = pl.BlockSpec((1, MIX_HALF), lambda b, t: (0, 0))
    hu_spec = pl.BlockSpec((None, 1, POOL_HIST, MIX_HALF), lambda b, t: (io if hist_u.shape[0] > 1 else 0, b, 0, 0))
    hg_spec = pl.BlockSpec((None, 1, CONV_HIST, MIX_HALF), lambda b, t: (io if hist_g.shape[0] > 1 else 0, b, 0, 0))
    return pl.pallas_call(
        functools.partial(_odd_kernel, tm=tm, q_start=q_start),
        grid=(bx, tx // tm),
        in_specs=[tile, tile, tile, hu_spec, hg_spec,
                  pl.BlockSpec((None, len(POOL_WINDOWS), POOL_GC, POOL_GC), lambda b, t: (io, 0, 0, 0)),
                  vec,
                  pl.BlockSpec((None, CONV_WIDTH, MIX_HALF), lambda b, t: (io, 0, 0)),
                  vec, vec, vec],
        out_specs=[tile, tile,
                   pl.BlockSpec((1, POOL_HIST, MIX_HALF), lambda b, t: (b, 0, 0)),
                   pl.BlockSpec((1, CONV_HIST, MIX_HALF), lambda b, t: (b, 0, 0))],
        out_shape=[jax.ShapeDtypeStruct((bx, tx, MIX_HALF), F32), jax.ShapeDtypeStruct((bx, tx, MIX_HALF), F32),
                   jax.ShapeDtypeStruct((bx, POOL_HIST, MIX_HALF), F32),
                   jax.ShapeDtypeStruct((bx, CONV_HIST, MIX_HALF), F32)],
        scratch_shapes=[pltpu.VMEM((2 * SUBLANES + tm, MIX_HALF), F32),
                        pltpu.VMEM((4 * SUBLANES + tm, MIX_HALF), F32)],
        compiler_params=_cparams("arbitrary", "arbitrary"),
        name="odd_mixer",
    )(u, a, gt, hist_u, hist_g, pool_w, pool_scale[io:io + 1], conv_w, conv_b[io:io + 1], ln_g[io:io + 1],
      ln_b[io:io + 1])


def _post_kernel(a_ref, b_ref, x_ref, g1_ref, sh2_ref, sc2_ref, g2_ref, ng_ref, woa_ref, wob_ref,
                 w1_ref, w2_ref, o_ref, x1_scr, h2_scr, acc_scr):
    c = pl.program_id(2)

    @pl.when(c == 0)
    def _():
        m = _dot(a_ref[0].astype(BF16), woa_ref[...]) + _dot(b_ref[0].astype(BF16), wob_ref[...])
        x1 = x_ref[0] + g1_ref[0] * _rms(m, ng_ref[1:2, :])
        x1_scr[...] = x1
        h2_scr[...] = (_rms(x1, ng_ref[2:3, :]) * (1.0 + sc2_ref[0]) + sh2_ref[0]).astype(BF16)
        acc_scr[...] = jnp.zeros_like(acc_scr)

    hc = jnp.maximum(_dot(h2_scr[...], w1_ref[...]), 0.0)
    acc_scr[...] += _dot((hc * hc).astype(BF16), w2_ref[...])

    @pl.when(c == pl.num_programs(2) - 1)
    def _():
        o_ref[0] = x1_scr[...] + g2_ref[0] * _rms(acc_scr[...], ng_ref[3:4, :])


def _post(a, b, x, mod, norm_g_l, wo_bf16, w1_bf16, w2_bf16):
    bx, tx, _ = x.shape
    tm = min(tx, ROW_TILE)
    half = pl.BlockSpec((1, tm, MIX_HALF), lambda b_, t, c: (b_, t, 0))
    full = pl.BlockSpec((1, tm, D_MODEL), lambda b_, t, c: (b_, t, 0))
    return pl.pallas_call(
        _post_kernel,
        grid=(bx, tx // tm, D_FF // FF_TILE),
        in_specs=[half, half, full,
                  _mod_spec(mod, tm, 2), _mod_spec(mod, tm, 3), _mod_spec(mod, tm, 4), _mod_spec(mod, tm, 5),
                  pl.BlockSpec((4, D_MODEL), lambda b_, t, c: (0, 0)),
                  pl.BlockSpec((MIX_HALF, D_MODEL), lambda b_, t, c: (0, 0)),
                  pl.BlockSpec((MIX_HALF, D_MODEL), lambda b_, t, c: (1, 0)),
                  pl.BlockSpec((D_MODEL, FF_TILE), lambda b_, t, c: (0, c)),
                  pl.BlockSpec((FF_TILE, D_MODEL), lambda b_, t, c: (c, 0))],
        out_specs=full,
        out_shape=jax.ShapeDtypeStruct((bx, tx, D_MODEL), F32),
        scratch_shapes=[pltpu.VMEM((tm, D_MODEL), F32), pltpu.VMEM((tm, D_MODEL), BF16),
                        pltpu.VMEM((tm, D_MODEL), F32)],
        compiler_params=_cparams("arbitrary", "arbitrary", "arbitrary"),
        name="post",
    )(a, b, x, mod, mod, mod, mod, norm_g_l, wo_bf16, wo_bf16, w1_bf16, w2_bf16)


def kernel(x_prompt, x_sample, c_prompt, c_sample, cache_sb_k, cache_sb_v, state_hgrn, state_pool, state_conv,
           page_table, norm_g, ada_w, ada_b, even_w_in, even_w_out, sb_bias, hgrn_lb_logits, hgrn_norm_g,
           odd_w_in, odd_w_out, pool_w, pool_scale, conv_w, conv_b, conv_ln_g, conv_ln_b, mlp_w1, mlp_w2):
    n_even = even_w_in.shape[0]
    bp, tp, _ = x_prompt.shape
    bs, ts, _ = x_sample.shape
    n_pool, page = cache_sb_k.shape[1], cache_sb_k.shape[2]
    past_len = page_table.shape[1] * page

    lb_cum = jnp.cumsum(jax.nn.softmax(hgrn_lb_logits.astype(F32), axis=0), axis=0)
    lower_bounds = jnp.maximum(lb_cum - lb_cum[:1], 0.0)

    mod_all = _ada(jnp.concatenate([c_prompt, c_sample], axis=0), ada_w, ada_b)
    even_w_in_b = even_w_in.astype(BF16)
    even_w_out_b = even_w_out.astype(BF16)
    odd_w_in_b = odd_w_in.astype(BF16)
    odd_w_out_b = odd_w_out.astype(BF16)
    w1_b = mlp_w1.astype(BF16)
    w2_b = mlp_w2.astype(BF16)
    cache_k = cache_sb_k.transpose(0, 1, 3, 4, 2).reshape(n_even, n_pool, MIX_HALF, page)
    cache_v = cache_sb_v.transpose(0, 1, 3, 4, 2).reshape(n_even, n_pool, MIX_HALF, page)
    gn = hgrn_norm_g.reshape(n_even, MIX_HALF)

    def run(x, mods, is_sample):
        bx, tx, _ = x.shape
        s_list, k_list, v_list, p_list, c_list = [], [], [], [], []
        for l in range(DEPTH):
            mod = mods[l]
            if l % 2 == 0:
                ie = l // 2
                qfig, q_b, k_b, v_b = _in_proj(x, mod, norm_g[l], even_w_in_b[ie],
                                               (4 * MIX_HALF, MIX_HALF, MIX_HALF, MIX_HALF))
                if is_sample:
                    hb, ht = bs, ts
                    s0 = state_hgrn[ie]
                else:
                    hb, ht = bx, tx
                    s0 = jnp.zeros((bx, HG_HEADS, HG_D, HG_D), F32)
                o_a, s_new = _hgrn(qfig.reshape(hb, ht, 4 * MIX_HALF), lower_bounds[ie:ie + 1], gn[ie:ie + 1], s0)
                if is_sample:
                    bias_rows = jnp.repeat(sb_bias[ie].astype(F32), ts).reshape(SB_HEADS * ts, 1)
                    o_b = _sb_sample(ie, page_table, bias_rows, q_b.reshape(bs, ts, MIX_HALF),
                                     k_b.reshape(bs, ts, MIX_HALF), v_b.reshape(bs, ts, MIX_HALF), cache_k, cache_v)
                else:
                    o_b = _sb_prompt(sb_bias[ie].astype(F32), q_b, k_b, v_b)
                mix_a, mix_b, w_out = o_a.reshape(bx, tx, MIX_HALF), o_b.reshape(bx, tx, MIX_HALF), even_w_out_b[ie]
                s_list.append(s_new)
                k_list.append(k_b.reshape(hb, ht, SB_HEADS, SB_HD))
                v_list.append(v_b.reshape(hb, ht, SB_HEADS, SB_HD))
            else:
                io = l // 2
                u, a, gt = _in_proj(x, mod, norm_g[l], odd_w_in_b[io], (MIX_HALF, MIX_HALF, MIX_HALF))
                if is_sample:
                    hb, ht = bs, ts
                    hist_u, hist_g, q_start = state_pool, state_conv, past_len
                else:
                    hb, ht = bx, tx
                    hist_u = jnp.zeros((1, bx, POOL_HIST, MIX_HALF), F32)
                    hist_g = jnp.zeros((1, bx, CONV_HIST, MIX_HALF), F32)
                    q_start = 0
                po, co, new_u, new_g = _odd(u.reshape(hb, ht, MIX_HALF), a.reshape(hb, ht, MIX_HALF),
                                            gt.reshape(hb, ht, MIX_HALF), hist_u, hist_g, io, pool_w, pool_scale,
                                            conv_w, conv_b, conv_ln_g, conv_ln_b, q_start)
                mix_a, mix_b, w_out = po.reshape(bx, tx, MIX_HALF), co.reshape(bx, tx, MIX_HALF), odd_w_out_b[io]
                p_list.append(new_u)
                c_list.append(new_g)
            x = _post(mix_a, mix_b, x, mod, norm_g[l], w_out, w1_b[l], w2_b[l])
        return (x, jnp.stack(s_list), jnp.stack(k_list), jnp.stack(v_list), jnp.stack(p_list), jnp.stack(c_list))

    mods_p = [mod_all[l, :bp].reshape(bp, 1, 6 * D_MODEL) for l in range(DEPTH)]
    mods_s = [jnp.repeat(mod_all[l, bp:], ts, axis=0).reshape(1, bs * ts, 6 * D_MODEL) for l in range(DEPTH)]
    y_p, s_p, k_p, v_p, pool_p, conv_p = run(x_prompt, mods_p, False)
    y_s, s_s, k_s, v_s, pool_s, conv_s = run(x_sample.reshape(1, bs * ts, D_MODEL), mods_s, True)
    return (y_p, y_s.reshape(bs, ts, D_MODEL), k_p, v_p, k_s, v_s, s_p, s_s, pool_p, pool_s, conv_p, conv_s)
```

```python
import functools

import jax
import jax.numpy as jnp
from jax import lax
from jax.experimental import pallas as pl
from jax.experimental.pallas import tpu as pltpu

F32 = jnp.float32
BF16 = jnp.bfloat16

D_MODEL = 1024
DEPTH = 4
MIX_HALF = D_MODEL // 2
HG_HEADS = 4
HG_D = MIX_HALF // HG_HEADS
SB_HEADS = 8
SB_HD = MIX_HALF // SB_HEADS
SB_SCALE = SB_HD ** -0.5
POOL_WINDOWS = (2, 4, 8, 16)
POOL_GC = MIX_HALF // len(POOL_WINDOWS)
POOL_HIST = max(POOL_WINDOWS) - 1
CONV_WIDTH = 31
CONV_HIST = CONV_WIDTH - 1
D_FF = 4 * D_MODEL
EPS = 1e-6

LANES = 128
SUBLANES = 8
VMEM_LIMIT = 52 * 1024 * 1024

ROW_TILE = 512
FF_TILE = 1024
HG_CHUNK = 128
HG_SUB = 32
HG_SAFE_DECAY = 60.0
HG_DIRECT_SUB = 16
SB_BLOCK = 256
SB_PAIRS = 2


def _cparams(*sem):
    return pltpu.CompilerParams(dimension_semantics=sem, vmem_limit_bytes=VMEM_LIMIT)


def _dot(a, b):
    return jnp.dot(a, b, preferred_element_type=F32)


def _dot_nt(a, b):
    return lax.dot_general(a, b, (((1,), (1,)), ((), ())), preferred_element_type=F32)


def _split_dot(x, m_bf16, terms):
    acc = None
    r = x
    for i in range(terms):
        p = r.astype(BF16)
        d = _dot(p, m_bf16)
        acc = d if acc is None else acc + d
        if i + 1 < terms:
            r = r - p.astype(F32)
    return acc


def _split_dot_left(m_bf16, x, terms):
    acc = None
    r = x
    for i in range(terms):
        p = r.astype(BF16)
        d = _dot(m_bf16, p)
        acc = d if acc is None else acc + d
        if i + 1 < terms:
            r = r - p.astype(F32)
    return acc


def _sigmoid(x):
    return 1.0 / (1.0 + jnp.exp(-x))


def _log_sigmoid(x):
    return jnp.minimum(x, 0.0) - jnp.log1p(jnp.exp(-jnp.abs(x)))


def _log_sigmoid_pair(z):
    ls = jnp.minimum(z, 0.0) - jnp.log(1.0 + jnp.exp(-jnp.abs(z)))
    return ls, ls - z


def _rms(x, g):
    return x * lax.rsqrt(jnp.mean(x * x, axis=-1, keepdims=True) + EPS) * g


def _ada_kernel(c_ref, w_ref, b_ref, o_ref):
    c = c_ref[...]
    s = (c * _sigmoid(c)).astype(BF16)
    o_ref[0] = _dot(s, w_ref[0].astype(BF16)) + b_ref[0]


def _ada(c_all, ada_w, ada_b):
    nb = c_all.shape[0]
    tn = 1536
    return pl.pallas_call(
        _ada_kernel,
        grid=(DEPTH, 6 * D_MODEL // tn),
        in_specs=[
            pl.BlockSpec((nb, D_MODEL), lambda l, j: (0, 0)),
            pl.BlockSpec((1, D_MODEL, tn), lambda l, j: (l, 0, j)),
            pl.BlockSpec((1, 1, tn), lambda l, j: (l, 0, j)),
        ],
        out_specs=pl.BlockSpec((1, nb, tn), lambda l, j: (l, 0, j)),
        out_shape=jax.ShapeDtypeStruct((DEPTH, nb, 6 * D_MODEL), F32),
        compiler_params=_cparams("arbitrary", "arbitrary"),
        name="ada",
    )(c_all, ada_w, ada_b.reshape(DEPTH, 1, 6 * D_MODEL))


def _mod_spec(mod, tm, chunk):
    if mod.shape[1] == 1:
        return pl.BlockSpec((1, 1, D_MODEL), lambda b, t, *_: (b, 0, chunk))
    return pl.BlockSpec((1, tm, D_MODEL), lambda b, t, *_: (b, t, chunk))


def _inproj_kernel(x_ref, sh_ref, sc_ref, g_ref, w_ref, *o_refs, plan):
    h = _rms(x_ref[0], g_ref[0:1, :]) * (1.0 + sc_ref[0]) + sh_ref[0]
    y = _dot(h.astype(BF16), w_ref[...])
    transposed = {}
    for o_ref, (c0, n, kind) in zip(o_refs, plan):
        blk = y[:, c0:c0 + n]
        if kind == "rows":
            o_ref[0] = blk
        elif kind == "rows_q":
            o_ref[0] = (blk * SB_SCALE).astype(BF16)
        else:
            if c0 not in transposed:
                transposed[c0] = blk.T
            o_ref[0] = transposed[c0].astype(o_ref.dtype)


def _in_proj(x, mod, norm_g_l, w_bf16, plan):
    bx, tx, _ = x.shape
    tm = min(tx, ROW_TILE)
    n = w_bf16.shape[1]
    out_specs, out_shape = [], []
    for _, wd, kind in plan:
        if kind in ("rows", "rows_q"):
            out_specs.append(pl.BlockSpec((1, tm, wd), lambda b, t: (b, t, 0)))
            out_shape.append(jax.ShapeDtypeStruct((bx, tx, wd), F32 if kind == "rows" else BF16))
        else:
            out_specs.append(pl.BlockSpec((1, wd, tm), lambda b, t: (b, 0, t)))
            out_shape.append(jax.ShapeDtypeStruct((bx, wd, tx), F32 if kind == "cols" else BF16))
    return pl.pallas_call(
        functools.partial(_inproj_kernel, plan=tuple(plan)),
        grid=(bx, tx // tm),
        in_specs=[
            pl.BlockSpec((1, tm, D_MODEL), lambda b, t: (b, t, 0)),
            _mod_spec(mod, tm, 0),
            _mod_spec(mod, tm, 1),
            pl.BlockSpec((4, D_MODEL), lambda b, t: (0, 0)),
            pl.BlockSpec((D_MODEL, n), lambda b, t: (0, 0)),
        ],
        out_specs=out_specs,
        out_shape=out_shape,
        compiler_params=_cparams("arbitrary", "arbitrary"),
        name="in_proj",
    )(x, mod, mod, norm_g_l, w_bf16)


def _hgrn_direct(h, q_ref, i_ref, b_scr, kk_scr, oi_scr, o_scr, *, chunk):
    sub = min(chunk, HG_DIRECT_SUB)
    sl = slice(h * HG_D, (h + 1) * HG_D)
    rows = lax.broadcasted_iota(jnp.int32, (chunk, 1), 0)
    srow = lax.broadcasted_iota(jnp.int32, (sub, 1), 0)

    def sub_block(i, carry):
        r0 = pl.multiple_of(i * sub, sub)
        b_i = b_scr[pl.ds(r0, sub), sl]
        q_i = q_ref[0, pl.ds(r0, sub), sl]
        kk_i = kk_scr[pl.ds(r0, sub), sl]
        v_i = i_ref[0, pl.ds(r0, sub), sl]
        o = oi_scr[pl.ds(r0, sub), sl]
        if chunk > sub:
            anchor = b_i[0:1, :]
            qt = q_i * jnp.exp(b_i - anchor)
            kt = jnp.where(rows < r0, kk_scr[:, sl] * jnp.exp(jnp.minimum(anchor - b_scr[:, sl], 0.0)), 0.0)
            a_off = _dot_nt(qt.astype(BF16), kt.astype(BF16))
            o = o + _dot(a_off.astype(BF16), i_ref[0, :, sl].astype(BF16))
        o_diag = jnp.zeros((sub, HG_D), F32)
        for tt in range(sub):
            e = jnp.exp(jnp.minimum(b_i[tt:tt + 1, :] - b_i, 0.0))
            d = jnp.sum(q_i[tt:tt + 1, :] * e * kk_i, axis=-1, keepdims=True)
            d = jnp.where(srow <= tt, d, 0.0)
            row = jnp.sum(d * v_i, axis=0, keepdims=True)
            o_diag = jnp.where(srow == tt, row, o_diag)
        o_scr[pl.ds(r0, sub), sl] = o + o_diag
        return carry

    lax.fori_loop(0, chunk // sub, sub_block, 0)


def _hgrn_kernel(q_ref, f_ref, i_ref, g_ref, lb_ref, gn_ref, s0_ref, o_ref, sout_ref,
                 st_scr, b_scr, kk_scr, oi_scr, o_scr, *, chunk, sub, single, safe_decay):
    t = pl.program_id(1)
    if not single:
        @pl.when(t == 0)
        def _():
            for h in range(HG_HEADS):
                st_scr[h] = s0_ref[0, h].T

    lb = lb_ref[...]
    q = q_ref[0]
    fl = f_ref[0]
    v = i_ref[0]
    c = jnp.log1p(-lb) + _log_sigmoid(fl)
    a = jnp.log(lb)
    log_f = jnp.maximum(a, c) + jnp.log1p(jnp.exp(-jnp.abs(a - c)))
    kk = (1.0 - lb) * _sigmoid(-fl)
    ri = lax.broadcasted_iota(jnp.int32, (chunk, chunk), 0)
    ci = lax.broadcasted_iota(jnp.int32, (chunk, chunk), 1)
    tril = jnp.where(ri >= ci, 1.0, 0.0).astype(BF16)
    p1 = log_f.astype(BF16)
    r1 = log_f - p1.astype(F32)
    p2 = r1.astype(BF16)
    p3 = (r1 - p2.astype(F32)).astype(BF16)
    b3 = _dot(tril, jnp.concatenate([p1, p2, p3], axis=1))
    b = b3[:, 0:MIX_HALF] + b3[:, MIX_HALF:2 * MIX_HALF] + b3[:, 2 * MIX_HALF:3 * MIX_HALF]
    b_scr[...] = b
    kk_scr[...] = kk
    b_last = b[chunk - 1:chunk, :]
    qe = (q * jnp.exp(b)).astype(BF16)
    kd = kk * jnp.exp(b_last - b)
    dec = jnp.exp(b_last)
    npad = HG_D - chunk
    prow = lax.broadcasted_iota(jnp.int32, (HG_D, 1), 0)
    for h in range(HG_HEADS):
        sl = slice(h * HG_D, (h + 1) * HG_D)
        if single:
            s = s0_ref[0, h]
            oi_scr[:, sl] = _dot(qe[:, sl], s.astype(BF16))
            zpad = jnp.zeros((npad, HG_D), F32)
            m = jnp.concatenate([kd[:, sl], zpad], axis=0)
            m = jnp.where(prow == chunk, dec[:, sl], m)
            mt = m.T
            vpad = jnp.concatenate([v[:, sl], zpad], axis=0)
            sout_ref[0, h] = mt[:, chunk:chunk + 1] * s + _dot(mt.astype(BF16), vpad.astype(BF16))
        else:
            st = st_scr[h]
            oi_scr[:, sl] = _dot_nt(qe[:, sl], st.astype(BF16))
            st_scr[h] = dec[:, sl] * st + _dot(v[:, sl].T.astype(BF16), kd[:, sl].astype(BF16))

    n_sub = chunk // sub
    worst = jnp.zeros((1, MIX_HALF), F32)
    if single:
        zpad = jnp.zeros((npad, MIX_HALF), F32)
        kk_full = jnp.concatenate([kk, zpad], axis=0)
        b_full = jnp.concatenate([b, zpad], axis=0)
        v_full = jnp.concatenate([v, zpad], axis=0)
    else:
        kk_full, b_full, v_full = kk, b, v
    anchors = [b[i * sub:i * sub + 1, :] for i in range(n_sub)]
    for i in range(n_sub):
        worst = jnp.maximum(worst, anchors[i] - b[(i + 1) * sub - 1:(i + 1) * sub, :])
    anchor_rows = jnp.concatenate([jnp.broadcast_to(an, (sub, MIX_HALF)) for an in anchors], axis=0)
    qt = (q * jnp.exp(b - anchor_rows)).astype(BF16)
    kts = [(kk_full * jnp.exp(jnp.minimum(an - b_full, safe_decay))).astype(BF16) for an in anchors]
    row_blk = lax.broadcasted_iota(jnp.int32, (chunk, HG_D), 0) // sub
    causal = (lax.broadcasted_iota(jnp.int32, (chunk, HG_D), 1)
              <= lax.broadcasted_iota(jnp.int32, (chunk, HG_D), 0))
    zero_q = jnp.zeros((chunk, HG_D), BF16)
    atts = []
    for h in range(HG_HEADS):
        sl = slice(h * HG_D, (h + 1) * HG_D)
        q_cat = jnp.concatenate([jnp.where(row_blk == i, qt[:, sl], zero_q) for i in range(n_sub)], axis=1)
        k_cat = jnp.concatenate([kt[:, sl] for kt in kts], axis=1)
        atts.append(jnp.where(causal, _dot_nt(q_cat, k_cat), 0.0).astype(BF16))
    bd = (lax.broadcasted_iota(jnp.int32, (HG_HEADS * HG_D, MIX_HALF), 0) // HG_D
          == lax.broadcasted_iota(jnp.int32, (HG_HEADS * HG_D, MIX_HALF), 1) // HG_D)
    v_bd = jnp.where(bd, jnp.concatenate([v_full] * HG_HEADS, axis=0), 0.0).astype(BF16)
    o_scr[...] = oi_scr[...] + _dot(jnp.concatenate(atts, axis=1), v_bd)

    @pl.when(jnp.max(worst) > safe_decay)
    def _():
        for h in range(HG_HEADS):
            _hgrn_direct(h, q_ref, i_ref, b_scr, kk_scr, oi_scr, o_scr, chunk=chunk)

    g = g_ref[0]
    gn = gn_ref[...]
    for h in range(HG_HEADS):
        sl = slice(h * HG_D, (h + 1) * HG_D)
        gh = g[:, sl]
        o_ref[0, :, sl] = _rms(o_scr[:, sl], gn[:, sl]) * (gh * _sigmoid(gh))

    if not single:
        @pl.when(t == pl.num_programs(1) - 1)
        def _():
            for h in range(HG_HEADS):
                sout_ref[0, h] = st_scr[h].T


def _hgrn(qfig, lb_l, gn_l, s0, safe_decay=HG_SAFE_DECAY):
    bx, tx, _ = qfig.shape
    chunk = min(tx, HG_CHUNK)
    sub = min(chunk, HG_SUB)
    single = tx == chunk
    assert not single or chunk < HG_D

    def col(k):
        return pl.BlockSpec((1, chunk, MIX_HALF), lambda b, t: (b, t, k))

    vec = pl.BlockSpec((1, MIX_HALF), lambda b, t: (0, 0))
    state = pl.BlockSpec((1, HG_HEADS, HG_D, HG_D), lambda b, t: (b, 0, 0, 0))
    return pl.pallas_call(
        functools.partial(_hgrn_kernel, chunk=chunk, sub=sub, single=single, safe_decay=safe_decay),
        grid=(bx, tx // chunk),
        in_specs=[col(0), col(1), col(2), col(3), vec, vec, state],
        out_specs=[pl.BlockSpec((1, chunk, MIX_HALF), lambda b, t: (b, t, 0)), state],
        out_shape=[jax.ShapeDtypeStruct((bx, tx, MIX_HALF), F32),
                   jax.ShapeDtypeStruct((bx, HG_HEADS, HG_D, HG_D), F32)],
        scratch_shapes=[pltpu.VMEM((HG_HEADS, HG_D, HG_D), F32)] + [pltpu.VMEM((chunk, MIX_HALF), F32)] * 4,
        compiler_params=_cparams("arbitrary", "arbitrary"),
        name="hgrn",
    )(qfig, qfig, qfig, qfig, lb_l, gn_l, s0)


def _sb_prompt_kernel(bias_ref, q_ref, kt_ref, vt_ref, o_ref, acc_scr, carry_scr, *, blk):
    g = pl.program_id(1)
    qi = pl.program_id(2)
    lane = lax.broadcasted_iota(jnp.int32, (1, LANES), 1)
    ri = lax.broadcasted_iota(jnp.int32, (blk, blk), 0)
    ci = lax.broadcasted_iota(jnp.int32, (blk, blk), 1)
    upper = jnp.where(ri > ci, 1.0, 0.0).astype(BF16)
    tri = ci < ri
    causal = jnp.concatenate([tri, tri], axis=0)
    second = lax.broadcasted_iota(jnp.int32, (2 * blk, 1), 0) >= blk
    qs, bias = [], []
    for p in range(SB_PAIRS):
        q = q_ref[0, :, p * LANES:(p + 1) * LANES]
        zero = jnp.zeros_like(q)
        qs.append(jnp.concatenate([jnp.where(lane < SB_HD, q, zero), jnp.where(lane >= SB_HD, q, zero)], axis=0))
        h0 = 2 * (g * SB_PAIRS + p)
        bias.append(jnp.where(second, bias_ref[h0 + 1], bias_ref[h0]))
    acc_scr[...] = jnp.zeros_like(acc_scr)
    carry_scr[...] = jnp.zeros_like(carry_scr)

    def step(kb, masked):
        c0 = pl.multiple_of(kb * blk, blk)
        for p in range(SB_PAIRS):
            kblk = kt_ref[0, p * LANES:(p + 1) * LANES, pl.ds(c0, blk)]
            vblk = vt_ref[0, p * LANES:(p + 1) * LANES, pl.ds(c0, blk)]
            ls, lk = _log_sigmoid_pair(_dot(qs[p], kblk) + bias[p])
            if masked:
                lk = jnp.where(causal, lk, 0.0)
            hi = lk.astype(BF16)
            lo = (lk - hi.astype(F32)).astype(BF16)
            sfx = _dot(jnp.concatenate([hi, lo], axis=0), upper)
            carry = carry_scr[p, :, 0:1]
            w = jnp.exp(ls + (sfx[0:2 * blk] + sfx[2 * blk:4 * blk]) + carry)
            if masked:
                w = jnp.where(causal, w, 0.0)
            acc_scr[p] += _dot_nt(w.astype(BF16), vblk)
            carry_scr[p] = jnp.broadcast_to(carry + jnp.sum(lk, axis=-1, keepdims=True), (2 * blk, LANES))

    step(qi, True)

    def body(j, c):
        step(qi - 1 - j, False)
        return c

    lax.fori_loop(0, qi, body, 0)
    for p in range(SB_PAIRS):
        o_ref[0, :, p * LANES:(p + 1) * LANES] = jnp.where(lane < SB_HD, acc_scr[p, 0:blk], acc_scr[p, blk:2 * blk])


def _sb_prompt(bias_l, q_bf16, kt_bf16, vt_bf16):
    bx, tx, _ = q_bf16.shape
    blk = SB_BLOCK
    wd = SB_PAIRS * LANES
    rows = pl.BlockSpec((1, blk, wd), lambda b, g, qi: (b, qi, g))
    full = pl.BlockSpec((1, wd, tx), lambda b, g, qi: (b, g, 0))
    return pl.pallas_call(
        functools.partial(_sb_prompt_kernel, blk=blk),
        grid=(bx, MIX_HALF // wd, tx // blk),
        in_specs=[pl.BlockSpec(memory_space=pltpu.SMEM), rows, full, full],
        out_specs=rows,
        out_shape=jax.ShapeDtypeStruct((bx, tx, MIX_HALF), F32),
        scratch_shapes=[pltpu.VMEM((SB_PAIRS, 2 * blk, LANES), F32), pltpu.VMEM((SB_PAIRS, 2 * blk, LANES), F32)],
        compiler_params=_cparams("arbitrary", "arbitrary", "arbitrary"),
        name="sb_prompt",
    )(bias_l, q_bf16, kt_bf16, vt_bf16)


def _sb_sample_kernel(pt_ref, q_ref, kn_ref, vn_ref, bias_ref, *rest, tq, page, n_pages):
    kc_refs, vc_refs, o_ref = rest[:n_pages], rest[n_pages:2 * n_pages], rest[2 * n_pages]
    nrow = SB_HEADS * tq
    rowh = lax.broadcasted_iota(jnp.int32, (nrow, MIX_HALF), 0) // tq
    colh = lax.broadcasted_iota(jnp.int32, (nrow, MIX_HALF), 1) // SB_HD
    own = rowh == colh
    bias = bias_ref[...]
    ri = lax.broadcasted_iota(jnp.int32, (page, page), 0)
    ci = lax.broadcasted_iota(jnp.int32, (page, page), 1)
    upper = jnp.where(ri > ci, 1.0, 0.0).astype(BF16)
    qbd = (jnp.where(own, jnp.concatenate([q_ref[0]] * SB_HEADS, axis=0), 0.0) * SB_SCALE).astype(BF16)
    pad = jnp.zeros((page - tq, MIX_HALF), F32)
    kn = jnp.concatenate([kn_ref[0], pad], axis=0).astype(BF16)
    vn = jnp.concatenate([vn_ref[0], pad], axis=0).astype(BF16)
    qpos = lax.broadcasted_iota(jnp.int32, (nrow, page), 0) % tq
    kpos = lax.broadcasted_iota(jnp.int32, (nrow, page), 1)
    new_mask = kpos < qpos

    ls0, lk0 = _log_sigmoid_pair(_dot_nt(qbd, kn) + bias)
    ls_all, lk_all = [ls0], [jnp.where(new_mask, lk0, 0.0)]
    for j in range(n_pages):
        ls, lk = _log_sigmoid_pair(_dot(qbd, kc_refs[j][0, 0].astype(BF16)) + bias)
        ls_all.append(ls)
        lk_all.append(lk)
    lk_cat = jnp.concatenate(lk_all, axis=0)
    suffix = _split_dot(lk_cat, upper, 2)
    total = jnp.sum(lk_cat, axis=-1, keepdims=True)
    carry = jnp.zeros((nrow, 1), F32)
    acc = jnp.zeros((nrow, MIX_HALF), F32)
    for j in range(n_pages + 1):
        r = slice(j * nrow, (j + 1) * nrow)
        w = jnp.exp(ls_all[j] + suffix[r] + carry)
        if j == 0:
            acc = acc + _dot(jnp.where(new_mask, w, 0.0).astype(BF16), vn)
        else:
            acc = acc + _dot_nt(w.astype(BF16), vc_refs[j - 1][0, 0].astype(BF16))
        carry = carry + total[r]
    masked = jnp.where(own, acc, 0.0)
    out = masked[0:tq, :]
    for h in range(1, SB_HEADS):
        out = out + masked[h * tq:(h + 1) * tq, :]
    o_ref[0] = out


def _sb_sample(layer, page_table, bias_rows, q, k_new, v_new, cache_k, cache_v):
    bx, tq, _ = q.shape
    n_pages = page_table.shape[1]
    page = cache_k.shape[3]
    nrow = SB_HEADS * tq
    seq = pl.BlockSpec((1, tq, MIX_HALF), lambda b, pt: (b, 0, 0))

    def cache(j):
        return pl.BlockSpec((1, 1, MIX_HALF, page),
                            lambda b, pt: (layer, pt[b * n_pages + (n_pages - 1 - j)], 0, 0))

    pages = [cache(j) for j in range(n_pages)]
    return pl.pallas_call(
        functools.partial(_sb_sample_kernel, tq=tq, page=page, n_pages=n_pages),
        grid_spec=pltpu.PrefetchScalarGridSpec(
            num_scalar_prefetch=1,
            grid=(bx,),
            in_specs=[seq, seq, seq, pl.BlockSpec((nrow, 1), lambda b, pt: (0, 0))] + pages + pages,
            out_specs=seq,
        ),
        out_shape=jax.ShapeDtypeStruct((bx, tq, MIX_HALF), F32),
        compiler_params=_cparams("arbitrary"),
        name="sb_sample",
    )(page_table.reshape(-1), q, k_new, v_new, bias_rows, *([cache_k] * n_pages), *([cache_v] * n_pages))


def _odd_kernel(u_ref, a_ref, gt_ref, hu_ref, hg_ref, pw_ref, ps_ref, cw_ref, cb_ref, lg_ref, lbias_ref,
                po_ref, co_ref, nu_ref, ng_ref, eu_scr, eg_scr, *, tm, q_start):
    t = pl.program_id(1)
    hu0 = 2 * SUBLANES
    hg0 = 4 * SUBLANES

    @pl.when(t == 0)
    def _():
        eu_scr[0:hu0 - POOL_HIST, :] = jnp.zeros((hu0 - POOL_HIST, MIX_HALF), F32)
        eg_scr[0:hg0 - CONV_HIST, :] = jnp.zeros((hg0 - CONV_HIST, MIX_HALF), F32)
        eu_scr[hu0 - POOL_HIST:hu0, :] = hu_ref[0]
        eg_scr[hg0 - CONV_HIST:hg0, :] = hg_ref[0]

    @pl.when(t > 0)
    def _():
        eu_scr[0:hu0, :] = eu_scr[tm:tm + hu0, :]
        eg_scr[0:hg0, :] = eg_scr[tm:tm + hg0, :]

    u = u_ref[0]
    gt = gt_ref[0]
    glu = a_ref[0] * _sigmoid(gt)
    eu_scr[hu0:hu0 + tm, :] = u
    eg_scr[hg0:hg0 + tm, :] = glu

    pos = q_start + t * tm + lax.broadcasted_iota(jnp.int32, (tm, 1), 0)
    for gi, w in enumerate(POOL_WINDOWS):
        c0 = gi * POOL_GC
        wsum = u[:, c0:c0 + POOL_GC]
        for d in range(1, w):
            wsum = wsum + eu_scr[hu0 - d:hu0 - d + tm, c0:c0 + POOL_GC]
        cnt = jnp.minimum(w, pos + 1).astype(F32)
        pooled = wsum / cnt - u[:, c0:c0 + POOL_GC]
        po = _dot(pooled.astype(BF16), pw_ref[gi].astype(BF16))
        po_ref[0, :, c0:c0 + POOL_GC] = po * ps_ref[:, c0:c0 + POOL_GC]

    cv = jnp.broadcast_to(cb_ref[...], (tm, MIX_HALF))
    for j in range(CONV_WIDTH):
        s0 = hg0 - CONV_HIST + j
        cv = cv + cw_ref[j:j + 1, :] * eg_scr[s0:s0 + tm, :]
    mu = jnp.mean(cv, axis=-1, keepdims=True)
    dv = cv - mu
    var = jnp.mean(dv * dv, axis=-1, keepdims=True)
    y = dv * lax.rsqrt(var + EPS) * lg_ref[...] + lbias_ref[...]
    co_ref[0] = y * _sigmoid(y)

    @pl.when(t == pl.num_programs(1) - 1)
    def _():
        nu_ref[0] = eu_scr[hu0 + tm - POOL_HIST:hu0 + tm, :]
        ng_ref[0] = eg_scr[hg0 + tm - CONV_HIST:hg0 + tm, :]


def _odd(u, a, gt, hist_u, hist_g, io, pool_w, pool_scale, conv_w, conv_b, ln_g, ln_b, q_start):
    bx, tx, _ = u.shape
    tm = min(tx, 256)
    tile = pl.BlockSpec((1, tm, MIX_HALF), lambda b, t: (b, t, 0))
    vec = pl.BlockSpec((1, MIX_HALF), lambda b, t: (0, 0))
    hu_spec = pl.BlockSpec((None, 1, POOL_HIST, MIX_HALF), lambda b, t: (io if hist_u.shape[0] > 1 else 0, b, 0, 0))
    hg_spec = pl.BlockSpec((None, 1, CONV_HIST, MIX_HALF), lambda b, t: (io if hist_g.shape[0] > 1 else 0, b, 0, 0))
    return pl.pallas_call(
        functools.partial(_odd_kernel, tm=tm, q_start=q_start),
        grid=(bx, tx // tm),
        in_specs=[tile, tile, tile, hu_spec, hg_spec,
                  pl.BlockSpec((None, len(POOL_WINDOWS), POOL_GC, POOL_GC), lambda b, t: (io, 0, 0, 0)),
                  vec,
                  pl.BlockSpec((None, CONV_WIDTH, MIX_HALF), lambda b, t: (io, 0, 0)),
                  vec, vec, vec],
        out_specs=[tile, tile,
                   pl.BlockSpec((1, POOL_HIST, MIX_HALF), lambda b, t: (b, 0, 0)),
                   pl.BlockSpec((1, CONV_HIST, MIX_HALF), lambda b, t: (b, 0, 0))],
        out_shape=[jax.ShapeDtypeStruct((bx, tx, MIX_HALF), F32), jax.ShapeDtypeStruct((bx, tx, MIX_HALF), F32),
                   jax.ShapeDtypeStruct((bx, POOL_HIST, MIX_HALF), F32),
                   jax.ShapeDtypeStruct((bx, CONV_HIST, MIX_HALF), F32)],
        scratch_shapes=[pltpu.VMEM((2 * SUBLANES + tm, MIX_HALF), F32),
                        pltpu.VMEM((4 * SUBLANES + tm, MIX_HALF), F32)],
        compiler_params=_cparams("arbitrary", "arbitrary"),
        name="odd_mixer",
    )(u, a, gt, hist_u, hist_g, pool_w, pool_scale[io:io + 1], conv_w, conv_b[io:io + 1], ln_g[io:io + 1],
      ln_b[io:io + 1])


def _post_kernel(a_ref, b_ref, x_ref, g1_ref, sh2_ref, sc2_ref, g2_ref, ng_ref, woa_ref, wob_ref,
                 w1_ref, w2_ref, o_ref, x1_scr, h2_scr, acc_scr):
    c = pl.program_id(2)

    @pl.when(c == 0)
    def _():
        m = _dot(a_ref[0].astype(BF16), woa_ref[...]) + _dot(b_ref[0].astype(BF16), wob_ref[...])
        x1 = x_ref[0] + g1_ref[0] * _rms(m, ng_ref[1:2, :])
        x1_scr[...] = x1
        h2_scr[...] = (_rms(x1, ng_ref[2:3, :]) * (1.0 + sc2_ref[0]) + sh2_ref[0]).astype(BF16)
        acc_scr[...] = jnp.zeros_like(acc_scr)

    hc = jnp.maximum(_dot(h2_scr[...], w1_ref[...]), 0.0)
    acc_scr[...] += _dot((hc * hc).astype(BF16), w2_ref[...])

    @pl.when(c == pl.num_programs(2) - 1)
    def _():
        o_ref[0] = x1_scr[...] + g2_ref[0] * _rms(acc_scr[...], ng_ref[3:4, :])


def _post(a, b, x, mod, norm_g_l, wo_bf16, w1_bf16, w2_bf16):
    bx, tx, _ = x.shape
    tm = min(tx, ROW_TILE)
    half = pl.BlockSpec((1, tm, MIX_HALF), lambda b_, t, c: (b_, t, 0))
    full = pl.BlockSpec((1, tm, D_MODEL), lambda b_, t, c: (b_, t, 0))
    return pl.pallas_call(
        _post_kernel,
        grid=(bx, tx // tm, D_FF // FF_TILE),
        in_specs=[half, half, full,
                  _mod_spec(mod, tm, 2), _mod_spec(mod, tm, 3), _mod_spec(mod, tm, 4), _mod_spec(mod, tm, 5),
                  pl.BlockSpec((4, D_MODEL), lambda b_, t, c: (0, 0)),
                  pl.BlockSpec((MIX_HALF, D_MODEL), lambda b_, t, c: (0, 0)),
                  pl.BlockSpec((MIX_HALF, D_MODEL), lambda b_, t, c: (1, 0)),
                  pl.BlockSpec((D_MODEL, FF_TILE), lambda b_, t, c: (0, c)),
                  pl.BlockSpec((FF_TILE, D_MODEL), lambda b_, t, c: (c, 0))],
        out_specs=full,
        out_shape=jax.ShapeDtypeStruct((bx, tx, D_MODEL), F32),
        scratch_shapes=[pltpu.VMEM((tm, D_MODEL), F32), pltpu.VMEM((tm, D_MODEL), BF16),
                        pltpu.VMEM((tm, D_MODEL), F32)],
        compiler_params=_cparams("arbitrary", "arbitrary", "arbitrary"),
        name="post",
    )(a, b, x, mod, mod, mod, mod, norm_g_l, wo_bf16, wo_bf16, w1_bf16, w2_bf16)


def kernel(x_prompt, x_sample, c_prompt, c_sample, cache_sb_k, cache_sb_v, state_hgrn, state_pool, state_conv,
           page_table, norm_g, ada_w, ada_b, even_w_in, even_w_out, sb_bias, hgrn_lb_logits, hgrn_norm_g,
           odd_w_in, odd_w_out, pool_w, pool_scale, conv_w, conv_b, conv_ln_g, conv_ln_b, mlp_w1, mlp_w2):
    n_even = even_w_in.shape[0]
    bp, tp, _ = x_prompt.shape
    bs, ts, _ = x_sample.shape
    n_pool, page = cache_sb_k.shape[1], cache_sb_k.shape[2]
    past_len = page_table.shape[1] * page
    half = MIX_HALF

    lb_cum = jnp.cumsum(jax.nn.softmax(hgrn_lb_logits.astype(F32), axis=0), axis=0)
    lower_bounds = jnp.maximum(lb_cum - lb_cum[:1], 0.0)

    mod_all = _ada(jnp.concatenate([c_prompt, c_sample], axis=0), ada_w, ada_b)
    even_w_in_b = even_w_in.astype(BF16)
    even_w_out_b = even_w_out.astype(BF16)
    odd_w_in_b = odd_w_in.astype(BF16)
    odd_w_out_b = odd_w_out.astype(BF16)
    w1_b = mlp_w1.astype(BF16)
    w2_b = mlp_w2.astype(BF16)
    cache_k = cache_sb_k.transpose(0, 1, 3, 4, 2).reshape(n_even, n_pool, half, page)
    cache_v = cache_sb_v.transpose(0, 1, 3, 4, 2).reshape(n_even, n_pool, half, page)
    gn = hgrn_norm_g.reshape(n_even, half)

    def run(x, mods, is_sample):
        bx, tx, _ = x.shape
        s_list, k_list, v_list, p_list, c_list = [], [], [], [], []
        for l in range(DEPTH):
            mod = mods[l]
            if l % 2 == 0:
                ie = l // 2
                if is_sample:
                    qfig, q_b, k_b, v_b = _in_proj(
                        x, mod, norm_g[l], even_w_in_b[ie],
                        [(0, 4 * half, "rows"), (4 * half, half, "rows"), (5 * half, half, "rows"),
                         (6 * half, half, "rows")])
                    o_a, s_new = _hgrn(qfig.reshape(bs, ts, 4 * half), lower_bounds[ie:ie + 1], gn[ie:ie + 1],
                                       state_hgrn[ie])
                    bias_rows = jnp.repeat(sb_bias[ie].astype(F32), ts).reshape(SB_HEADS * ts, 1)
                    k_b = k_b.reshape(bs, ts, half)
                    v_b = v_b.reshape(bs, ts, half)
                    o_b = _sb_sample(ie, page_table, bias_rows, q_b.reshape(bs, ts, half), k_b, v_b, cache_k, cache_v)
                    k_list.append(k_b.reshape(bs, ts, SB_HEADS, SB_HD))
                    v_list.append(v_b.reshape(bs, ts, SB_HEADS, SB_HD))
                else:
                    qfig, q_b, k_t, k_tb, v_t, v_tb = _in_proj(
                        x, mod, norm_g[l], even_w_in_b[ie],
                        [(0, 4 * half, "rows"), (4 * half, half, "rows_q"), (5 * half, half, "cols"),
                         (5 * half, half, "cols_bf16"), (6 * half, half, "cols"), (6 * half, half, "cols_bf16")])
                    o_a, s_new = _hgrn(qfig, lower_bounds[ie:ie + 1], gn[ie:ie + 1],
                                       jnp.zeros((bx, HG_HEADS, HG_D, HG_D), F32))
                    o_b = _sb_prompt(sb_bias[ie].astype(F32), q_b, k_tb, v_tb)
                    k_list.append(k_t.reshape(bx, SB_HEADS, SB_HD, tx).transpose(0, 3, 1, 2))
                    v_list.append(v_t.reshape(bx, SB_HEADS, SB_HD, tx).transpose(0, 3, 1, 2))
                mix_a, mix_b, w_out = o_a.reshape(bx, tx, half), o_b.reshape(bx, tx, half), even_w_out_b[ie]
                s_list.append(s_new)
            else:
                io = l // 2
                u, a, gt = _in_proj(x, mod, norm_g[l], odd_w_in_b[io],
                                    [(0, half, "rows"), (half, half, "rows"), (2 * half, half, "rows")])
                if is_sample:
                    hb, ht = bs, ts
                    hist_u, hist_g, q_start = state_pool, state_conv, past_len
                else:
                    hb, ht = bx, tx
                    hist_u = jnp.zeros((1, bx, POOL_HIST, half), F32)
                    hist_g = jnp.zeros((1, bx, CONV_HIST, half), F32)
                    q_start = 0
                po, co, new_u, new_g = _odd(u.reshape(hb, ht, half), a.reshape(hb, ht, half),
                                            gt.reshape(hb, ht, half), hist_u, hist_g, io, pool_w, pool_scale,
                                            conv_w, conv_b, conv_ln_g, conv_ln_b, q_start)
                mix_a, mix_b, w_out = po.reshape(bx, tx, half), co.reshape(bx, tx, half), odd_w_out_b[io]
                p_list.append(new_u)
                c_list.append(new_g)
            x = _post(mix_a, mix_b, x, mod, norm_g[l], w_out, w1_b[l], w2_b[l])
        return (x, jnp.stack(s_list), jnp.stack(k_list), jnp.stack(v_list), jnp.stack(p_list), jnp.stack(c_list))

    mods_p = [mod_all[l, :bp].reshape(bp, 1, 6 * D_MODEL) for l in range(DEPTH)]
    mods_s = [jnp.repeat(mod_all[l, bp:], ts, axis=0).reshape(1, bs * ts, 6 * D_MODEL) for l in range(DEPTH)]
    y_p, s_p, k_p, v_p, pool_p, conv_p = run(x_prompt, mods_p, False)
    y_s, s_s, k_s, v_s, pool_s, conv_s = run(x_sample.reshape(1, bs * ts, D_MODEL), mods_s, True)
    return (y_p, y_s.reshape(bs, ts, D_MODEL), k_p, v_p, k_s, v_s, s_p, s_s, pool_p, pool_s, conv_p, conv_s)
```

```python
import functools

import jax
import jax.numpy as jnp
from jax import lax
from jax.experimental import pallas as pl
from jax.experimental.pallas import tpu as pltpu

F32 = jnp.float32
BF16 = jnp.bfloat16

D_MODEL = 1024
DEPTH = 4
MIX_HALF = D_MODEL // 2
HG_HEADS = 4
HG_D = MIX_HALF // HG_HEADS
SB_HEADS = 8
SB_HD = MIX_HALF // SB_HEADS
SB_SCALE = SB_HD ** -0.5
POOL_WINDOWS = (2, 4, 8, 16)
POOL_GC = MIX_HALF // len(POOL_WINDOWS)
POOL_HIST = max(POOL_WINDOWS) - 1
CONV_WIDTH = 31
CONV_HIST = CONV_WIDTH - 1
D_FF = 4 * D_MODEL
EPS = 1e-6

LANES = 128
SUBLANES = 8
VMEM_LIMIT = 52 * 1024 * 1024

ROW_TILE = 512
FF_TILE = 1024
HG_CHUNK = 128
HG_SUB = 16
HG_SAFE_DECAY = 72.0
HG_DIRECT_SUB = 16
SB_BLOCK = 256
SB_PAIRS = 2


def _cparams(*sem):
    return pltpu.CompilerParams(dimension_semantics=sem, vmem_limit_bytes=VMEM_LIMIT)


def _dot(a, b):
    return jnp.dot(a, b, preferred_element_type=F32)


def _dot_nt(a, b):
    return lax.dot_general(a, b, (((1,), (1,)), ((), ())), preferred_element_type=F32)


def _split_dot(x, m_bf16, terms):
    acc = None
    r = x
    for i in range(terms):
        p = r.astype(BF16)
        d = _dot(p, m_bf16)
        acc = d if acc is None else acc + d
        if i + 1 < terms:
            r = r - p.astype(F32)
    return acc


def _split_dot_left(m_bf16, x, terms):
    acc = None
    r = x
    for i in range(terms):
        p = r.astype(BF16)
        d = _dot(m_bf16, p)
        acc = d if acc is None else acc + d
        if i + 1 < terms:
            r = r - p.astype(F32)
    return acc


def _sigmoid(x):
    return 1.0 / (1.0 + jnp.exp(-x))


def _log_sigmoid(x):
    return jnp.minimum(x, 0.0) - jnp.log1p(jnp.exp(-jnp.abs(x)))


def _log_sigmoid_pair(z):
    ls = jnp.minimum(z, 0.0) - jnp.log(1.0 + jnp.exp(-jnp.abs(z)))
    return ls, ls - z


def _rms(x, g):
    return x * lax.rsqrt(jnp.mean(x * x, axis=-1, keepdims=True) + EPS) * g


def _ada_kernel(c_ref, w_ref, b_ref, o_ref):
    c = c_ref[...]
    s = (c * _sigmoid(c)).astype(BF16)
    o_ref[0] = _dot(s, w_ref[0].astype(BF16)) + b_ref[0]


def _ada(c_all, ada_w, ada_b):
    nb = c_all.shape[0]
    tn = 1536
    return pl.pallas_call(
        _ada_kernel,
        grid=(DEPTH, 6 * D_MODEL // tn),
        in_specs=[
            pl.BlockSpec((nb, D_MODEL), lambda l, j: (0, 0)),
            pl.BlockSpec((1, D_MODEL, tn), lambda l, j: (l, 0, j)),
            pl.BlockSpec((1, 1, tn), lambda l, j: (l, 0, j)),
        ],
        out_specs=pl.BlockSpec((1, nb, tn), lambda l, j: (l, 0, j)),
        out_shape=jax.ShapeDtypeStruct((DEPTH, nb, 6 * D_MODEL), F32),
        compiler_params=_cparams("arbitrary", "arbitrary"),
        name="ada",
    )(c_all, ada_w, ada_b.reshape(DEPTH, 1, 6 * D_MODEL))


def _mod_spec(mod, tm, chunk):
    if mod.shape[1] == 1:
        return pl.BlockSpec((1, 1, D_MODEL), lambda b, t, *_: (b, 0, chunk))
    return pl.BlockSpec((1, tm, D_MODEL), lambda b, t, *_: (b, t, chunk))


def _inproj_kernel(x_ref, sh_ref, sc_ref, g_ref, w_ref, *o_refs, plan):
    h = _rms(x_ref[0], g_ref[0:1, :]) * (1.0 + sc_ref[0]) + sh_ref[0]
    y = _dot(h.astype(BF16), w_ref[...])
    transposed = {}
    for o_ref, (c0, n, kind) in zip(o_refs, plan):
        blk = y[:, c0:c0 + n]
        if kind == "rows":
            o_ref[0] = blk
        elif kind == "rows_q":
            o_ref[0] = (blk * SB_SCALE).astype(BF16)
        else:
            if c0 not in transposed:
                transposed[c0] = blk.T
            o_ref[0] = transposed[c0].astype(o_ref.dtype)


def _in_proj(x, mod, norm_g, layer, w_bf16, w_layer, plan):
    bx, tx, _ = x.shape
    tm = min(tx, ROW_TILE)
    n = w_bf16.shape[2]
    out_specs, out_shape = [], []
    for _, wd, kind in plan:
        if kind in ("rows", "rows_q"):
            out_specs.append(pl.BlockSpec((1, tm, wd), lambda b, t: (b, t, 0)))
            out_shape.append(jax.ShapeDtypeStruct((bx, tx, wd), F32 if kind == "rows" else BF16))
        else:
            out_specs.append(pl.BlockSpec((1, wd, tm), lambda b, t: (b, 0, t)))
            out_shape.append(jax.ShapeDtypeStruct((bx, wd, tx), F32 if kind == "cols" else BF16))
    return pl.pallas_call(
        functools.partial(_inproj_kernel, plan=tuple(plan)),
        grid=(bx, tx // tm),
        in_specs=[
            pl.BlockSpec((1, tm, D_MODEL), lambda b, t: (b, t, 0)),
            _mod_spec(mod, tm, 0),
            _mod_spec(mod, tm, 1),
            pl.BlockSpec((None, 4, D_MODEL), lambda b, t: (layer, 0, 0)),
            pl.BlockSpec((None, D_MODEL, n), lambda b, t: (w_layer, 0, 0)),
        ],
        out_specs=out_specs,
        out_shape=out_shape,
        compiler_params=_cparams("arbitrary", "arbitrary"),
        name="in_proj",
    )(x, mod, mod, norm_g, w_bf16)


def _hgrn_direct(h, q_ref, i_ref, b_scr, kk_scr, oi_scr, o_scr, *, chunk):
    sub = min(chunk, HG_DIRECT_SUB)
    sl = slice(h * HG_D, (h + 1) * HG_D)
    rows = lax.broadcasted_iota(jnp.int32, (chunk, 1), 0)
    srow = lax.broadcasted_iota(jnp.int32, (sub, 1), 0)

    def sub_block(i, carry):
        r0 = pl.multiple_of(i * sub, sub)
        b_i = b_scr[pl.ds(r0, sub), sl]
        q_i = q_ref[0, pl.ds(r0, sub), sl]
        kk_i = kk_scr[pl.ds(r0, sub), sl]
        v_i = i_ref[0, pl.ds(r0, sub), sl]
        o = oi_scr[pl.ds(r0, sub), sl]
        if chunk > sub:
            anchor = b_i[0:1, :]
            qt = q_i * jnp.exp(b_i - anchor)
            kt = jnp.where(rows < r0, kk_scr[:, sl] * jnp.exp(jnp.minimum(anchor - b_scr[:, sl], 0.0)), 0.0)
            a_off = _dot_nt(qt.astype(BF16), kt.astype(BF16))
            o = o + _dot(a_off.astype(BF16), i_ref[0, :, sl].astype(BF16))
        o_diag = jnp.zeros((sub, HG_D), F32)
        for tt in range(sub):
            e = jnp.exp(jnp.minimum(b_i[tt:tt + 1, :] - b_i, 0.0))
            d = jnp.sum(q_i[tt:tt + 1, :] * e * kk_i, axis=-1, keepdims=True)
            d = jnp.where(srow <= tt, d, 0.0)
            row = jnp.sum(d * v_i, axis=0, keepdims=True)
            o_diag = jnp.where(srow == tt, row, o_diag)
        o_scr[pl.ds(r0, sub), sl] = o + o_diag
        return carry

    lax.fori_loop(0, chunk // sub, sub_block, 0)


def _hgrn_kernel(q_ref, f_ref, i_ref, g_ref, lb_ref, gn_ref, s0_ref, o_ref, sout_ref,
                 st_scr, b_scr, kk_scr, oi_scr, o_scr, *, chunk, sub, single, safe_decay):
    t = pl.program_id(1)
    if not single:
        @pl.when(t == 0)
        def _():
            for h in range(HG_HEADS):
                st_scr[h] = s0_ref[0, h].T

    lb = lb_ref[...]
    q = q_ref[0]
    fl = f_ref[0]
    v = i_ref[0]
    c = jnp.log1p(-lb) + _log_sigmoid(fl)
    a = jnp.log(lb)
    log_f = jnp.maximum(a, c) + jnp.log1p(jnp.exp(-jnp.abs(a - c)))
    kk = (1.0 - lb) * _sigmoid(-fl)
    ri = lax.broadcasted_iota(jnp.int32, (chunk, chunk), 0)
    ci = lax.broadcasted_iota(jnp.int32, (chunk, chunk), 1)
    tril = jnp.where(ri >= ci, 1.0, 0.0).astype(BF16)
    p1 = log_f.astype(BF16)
    r1 = log_f - p1.astype(F32)
    p2 = r1.astype(BF16)
    p3 = (r1 - p2.astype(F32)).astype(BF16)
    b3 = _dot(tril, jnp.concatenate([p1, p2, p3], axis=1))
    b = b3[:, 0:MIX_HALF] + b3[:, MIX_HALF:2 * MIX_HALF] + b3[:, 2 * MIX_HALF:3 * MIX_HALF]
    b_scr[...] = b
    kk_scr[...] = kk
    b_last = b[chunk - 1:chunk, :]
    qe = (q * jnp.exp(b)).astype(BF16)
    kd = kk * jnp.exp(b_last - b)
    dec = jnp.exp(b_last)
    npad = HG_D - chunk
    prow = lax.broadcasted_iota(jnp.int32, (HG_D, 1), 0)
    for h in range(HG_HEADS):
        sl = slice(h * HG_D, (h + 1) * HG_D)
        if single:
            s = s0_ref[0, h]
            oi_scr[:, sl] = _dot(qe[:, sl], s.astype(BF16))
            zpad = jnp.zeros((npad, HG_D), F32)
            m = jnp.concatenate([kd[:, sl], zpad], axis=0)
            m = jnp.where(prow == chunk, dec[:, sl], m)
            mt = m.T
            vpad = jnp.concatenate([v[:, sl], zpad], axis=0)
            sout_ref[0, h] = mt[:, chunk:chunk + 1] * s + _dot(mt.astype(BF16), vpad.astype(BF16))
        else:
            st = st_scr[h]
            oi_scr[:, sl] = _dot_nt(qe[:, sl], st.astype(BF16))
            st_scr[h] = dec[:, sl] * st + _dot(v[:, sl].T.astype(BF16), kd[:, sl].astype(BF16))

    n_sub = chunk // sub
    worst = jnp.zeros((1, MIX_HALF), F32)
    if single:
        zpad = jnp.zeros((npad, MIX_HALF), F32)
        kk_full = jnp.concatenate([kk, zpad], axis=0)
        b_full = jnp.concatenate([b, zpad], axis=0)
        v_full = jnp.concatenate([v, zpad], axis=0)
    else:
        kk_full, b_full, v_full = kk, b, v
    anchors = [b[i * sub:i * sub + 1, :] for i in range(n_sub)]
    for i in range(n_sub):
        worst = jnp.maximum(worst, anchors[i] - b[(i + 1) * sub - 1:(i + 1) * sub, :])
    anchor_rows = jnp.concatenate([jnp.broadcast_to(an, (sub, MIX_HALF)) for an in anchors], axis=0)
    qt = (q * jnp.exp(b - anchor_rows)).astype(BF16)
    kts = []
    for i, an in enumerate(anchors):
        n = kk_full.shape[0] if single else (i + 1) * sub
        kt = (kk_full[0:n] * jnp.exp(jnp.minimum(an - b_full[0:n], safe_decay))).astype(BF16)
        if n < kk_full.shape[0]:
            kt = jnp.concatenate([kt, jnp.zeros((kk_full.shape[0] - n, MIX_HALF), BF16)], axis=0)
        kts.append(kt)
    row_blk = lax.broadcasted_iota(jnp.int32, (chunk, HG_D), 0) // sub
    causal = (lax.broadcasted_iota(jnp.int32, (chunk, HG_D), 1)
              <= lax.broadcasted_iota(jnp.int32, (chunk, HG_D), 0))
    zero_q = jnp.zeros((chunk, HG_D), BF16)
    atts = []
    for h in range(HG_HEADS):
        sl = slice(h * HG_D, (h + 1) * HG_D)
        q_cat = jnp.concatenate([jnp.where(row_blk == i, qt[:, sl], zero_q) for i in range(n_sub)], axis=1)
        k_cat = jnp.concatenate([kt[:, sl] for kt in kts], axis=1)
        atts.append(jnp.where(causal, _dot_nt(q_cat, k_cat), 0.0).astype(BF16))
    bd = (lax.broadcasted_iota(jnp.int32, (HG_HEADS * HG_D, MIX_HALF), 0) // HG_D
          == lax.broadcasted_iota(jnp.int32, (HG_HEADS * HG_D, MIX_HALF), 1) // HG_D)
    v_bd = jnp.where(bd, jnp.concatenate([v_full] * HG_HEADS, axis=0), 0.0).astype(BF16)
    o_scr[...] = oi_scr[...] + _dot(jnp.concatenate(atts, axis=1), v_bd)

    @pl.when(jnp.max(worst) > safe_decay)
    def _():
        for h in range(HG_HEADS):
            _hgrn_direct(h, q_ref, i_ref, b_scr, kk_scr, oi_scr, o_scr, chunk=chunk)

    g = g_ref[0]
    gn = gn_ref[...]
    for h in range(HG_HEADS):
        sl = slice(h * HG_D, (h + 1) * HG_D)
        gh = g[:, sl]
        o_ref[0, :, sl] = _rms(o_scr[:, sl], gn[:, sl]) * (gh * _sigmoid(gh))

    if not single:
        @pl.when(t == pl.num_programs(1) - 1)
        def _():
            for h in range(HG_HEADS):
                sout_ref[0, h] = st_scr[h].T


def _hgrn(qfig, lb_l, gn_l, s0, s0_layer=0, safe_decay=HG_SAFE_DECAY):
    bx, tx, _ = qfig.shape
    chunk = min(tx, HG_CHUNK)
    sub = min(chunk, HG_SUB)
    single = tx == chunk
    assert not single or chunk < HG_D

    def col(k):
        return pl.BlockSpec((1, chunk, MIX_HALF), lambda b, t: (b, t, k))

    vec = pl.BlockSpec((1, MIX_HALF), lambda b, t: (0, 0))
    state = pl.BlockSpec((1, HG_HEADS, HG_D, HG_D), lambda b, t: (b, 0, 0, 0))
    state_in = pl.BlockSpec((None, 1, HG_HEADS, HG_D, HG_D), lambda b, t: (s0_layer, b, 0, 0, 0))
    return pl.pallas_call(
        functools.partial(_hgrn_kernel, chunk=chunk, sub=sub, single=single, safe_decay=safe_decay),
        grid=(bx, tx // chunk),
        in_specs=[col(0), col(1), col(2), col(3), vec, vec, state_in],
        out_specs=[pl.BlockSpec((1, chunk, MIX_HALF), lambda b, t: (b, t, 0)), state],
        out_shape=[jax.ShapeDtypeStruct((bx, tx, MIX_HALF), F32),
                   jax.ShapeDtypeStruct((bx, HG_HEADS, HG_D, HG_D), F32)],
        scratch_shapes=[pltpu.VMEM((HG_HEADS, HG_D, HG_D), F32)] + [pltpu.VMEM((chunk, MIX_HALF), F32)] * 4,
        compiler_params=_cparams("arbitrary", "arbitrary"),
        name="hgrn",
    )(qfig, qfig, qfig, qfig, lb_l, gn_l, s0)


def _sb_prompt_kernel(bias_ref, q_ref, kt_ref, vt_ref, o_ref, acc_scr, carry_scr, *, blk):
    g = pl.program_id(1)
    qi = pl.program_id(2)
    lane = lax.broadcasted_iota(jnp.int32, (1, LANES), 1)
    ri = lax.broadcasted_iota(jnp.int32, (blk, blk), 0)
    ci = lax.broadcasted_iota(jnp.int32, (blk, blk), 1)
    upper = jnp.where(ri > ci, 1.0, 0.0).astype(BF16)
    ones = jnp.ones((blk, LANES), BF16)
    tri = ci < ri
    causal = jnp.concatenate([tri, tri], axis=0)
    second = lax.broadcasted_iota(jnp.int32, (2 * blk, 1), 0) >= blk
    qs, bias = [], []
    for p in range(SB_PAIRS):
        q = q_ref[0, :, p * LANES:(p + 1) * LANES]
        zero = jnp.zeros_like(q)
        qs.append(jnp.concatenate([jnp.where(lane < SB_HD, q, zero), jnp.where(lane >= SB_HD, q, zero)], axis=0))
        h0 = 2 * (g * SB_PAIRS + p)
        bias.append(jnp.where(second, bias_ref[h0 + 1], bias_ref[h0]))
    acc_scr[...] = jnp.zeros_like(acc_scr)
    carry_scr[...] = jnp.zeros_like(carry_scr)

    def step(kb, masked):
        c0 = pl.multiple_of(kb * blk, blk)
        for p in range(SB_PAIRS):
            kblk = kt_ref[0, p * LANES:(p + 1) * LANES, pl.ds(c0, blk)]
            vblk = vt_ref[0, p * LANES:(p + 1) * LANES, pl.ds(c0, blk)]
            ls, lk = _log_sigmoid_pair(_dot(qs[p], kblk) + bias[p])
            if masked:
                lk = jnp.where(causal, lk, 0.0)
            lkb = lk.astype(BF16)
            carry = carry_scr[p]
            w = jnp.exp(ls + _dot(lkb, upper) + jnp.concatenate([carry] * (blk // LANES), axis=1))
            if masked:
                w = jnp.where(causal, w, 0.0)
            acc_scr[p] += _dot_nt(w.astype(BF16), vblk)
            carry_scr[p] = carry + _dot(lkb, ones)

    step(qi, True)

    def body(j, c):
        step(qi - 1 - j, False)
        return c

    lax.fori_loop(0, qi, body, 0)
    for p in range(SB_PAIRS):
        o_ref[0, :, p * LANES:(p + 1) * LANES] = jnp.where(lane < SB_HD, acc_scr[p, 0:blk], acc_scr[p, blk:2 * blk])


def _sb_prompt(bias_l, q_bf16, kt_bf16, vt_bf16):
    bx, tx, _ = q_bf16.shape
    blk = SB_BLOCK
    wd = SB_PAIRS * LANES
    rows = pl.BlockSpec((1, blk, wd), lambda b, g, qi: (b, qi, g))
    full = pl.BlockSpec((1, wd, tx), lambda b, g, qi: (b, g, 0))
    return pl.pallas_call(
        functools.partial(_sb_prompt_kernel, blk=blk),
        grid=(bx, MIX_HALF // wd, tx // blk),
        in_specs=[pl.BlockSpec(memory_space=pltpu.SMEM), rows, full, full],
        out_specs=rows,
        out_shape=jax.ShapeDtypeStruct((bx, tx, MIX_HALF), F32),
        scratch_shapes=[pltpu.VMEM((SB_PAIRS, 2 * blk, LANES), F32), pltpu.VMEM((SB_PAIRS, 2 * blk, LANES), F32)],
        compiler_params=_cparams("arbitrary", "arbitrary", "arbitrary"),
        name="sb_prompt",
    )(bias_l, q_bf16, kt_bf16, vt_bf16)


def _sb_sample_kernel(pt_ref, q_ref, kn_ref, vn_ref, bias_ref, *rest, tq, page, n_pages):
    kc_refs, vc_refs, o_ref = rest[:n_pages], rest[n_pages:2 * n_pages], rest[2 * n_pages]
    nrow = SB_HEADS * tq
    rowh = lax.broadcasted_iota(jnp.int32, (nrow, MIX_HALF), 0) // tq
    colh = lax.broadcasted_iota(jnp.int32, (nrow, MIX_HALF), 1) // SB_HD
    own = rowh == colh
    bias = bias_ref[...]
    ri = lax.broadcasted_iota(jnp.int32, (page, page), 0)
    ci = lax.broadcasted_iota(jnp.int32, (page, page), 1)
    upper = jnp.where(ri > ci, 1.0, 0.0).astype(BF16)
    qbd = (jnp.where(own, jnp.concatenate([q_ref[0]] * SB_HEADS, axis=0), 0.0) * SB_SCALE).astype(BF16)
    pad = jnp.zeros((page - tq, MIX_HALF), F32)
    kn = jnp.concatenate([kn_ref[0], pad], axis=0).astype(BF16)
    vn = jnp.concatenate([vn_ref[0], pad], axis=0).astype(BF16)
    qpos = lax.broadcasted_iota(jnp.int32, (nrow, page), 0) % tq
    kpos = lax.broadcasted_iota(jnp.int32, (nrow, page), 1)
    new_mask = kpos < qpos

    ls0, lk0 = _log_sigmoid_pair(_dot_nt(qbd, kn) + bias)
    ls_all, lk_all = [ls0], [jnp.where(new_mask, lk0, 0.0)]
    for j in range(n_pages):
        ls, lk = _log_sigmoid_pair(_dot(qbd, kc_refs[j][0, 0].astype(BF16)) + bias)
        ls_all.append(ls)
        lk_all.append(lk)
    lk_cat = jnp.concatenate(lk_all, axis=0)
    suffix = _split_dot(lk_cat, upper, 2)
    total = jnp.sum(lk_cat, axis=-1, keepdims=True)
    carry = jnp.zeros((nrow, 1), F32)
    acc = jnp.zeros((nrow, MIX_HALF), F32)
    for j in range(n_pages + 1):
        r = slice(j * nrow, (j + 1) * nrow)
        w = jnp.exp(ls_all[j] + suffix[r] + carry)
        if j == 0:
            acc = acc + _dot(jnp.where(new_mask, w, 0.0).astype(BF16), vn)
        else:
            acc = acc + _dot_nt(w.astype(BF16), vc_refs[j - 1][0, 0].astype(BF16))
        carry = carry + total[r]
    masked = jnp.where(own, acc, 0.0)
    out = masked[0:tq, :]
    for h in range(1, SB_HEADS):
        out = out + masked[h * tq:(h + 1) * tq, :]
    o_ref[0] = out


def _sb_sample(layer, page_table, bias_rows, q, k_new, v_new, cache_k, cache_v):
    bx, tq, _ = q.shape
    n_pages = page_table.shape[1]
    page = cache_k.shape[3]
    nrow = SB_HEADS * tq
    seq = pl.BlockSpec((1, tq, MIX_HALF), lambda b, pt: (b, 0, 0))

    def cache(j):
        return pl.BlockSpec((1, 1, MIX_HALF, page),
                            lambda b, pt: (layer, pt[b * n_pages + (n_pages - 1 - j)], 0, 0))

    pages = [cache(j) for j in range(n_pages)]
    return pl.pallas_call(
        functools.partial(_sb_sample_kernel, tq=tq, page=page, n_pages=n_pages),
        grid_spec=pltpu.PrefetchScalarGridSpec(
            num_scalar_prefetch=1,
            grid=(bx,),
            in_specs=[seq, seq, seq, pl.BlockSpec((nrow, 1), lambda b, pt: (0, 0))] + pages + pages,
            out_specs=seq,
        ),
        out_shape=jax.ShapeDtypeStruct((bx, tq, MIX_HALF), F32),
        compiler_params=_cparams("arbitrary"),
        name="sb_sample",
    )(page_table.reshape(-1), q, k_new, v_new, bias_rows, *([cache_k] * n_pages), *([cache_v] * n_pages))


def _odd_kernel(u_ref, a_ref, gt_ref, hu_ref, hg_ref, pw_ref, ps_ref, cw_ref, cb_ref, lg_ref, lbias_ref,
                po_ref, co_ref, nu_ref, ng_ref, eu_scr, eg_scr, *, tm, q_start):
    t = pl.program_id(1)
    hu0 = 2 * SUBLANES
    hg0 = 4 * SUBLANES

    @pl.when(t == 0)
    def _():
        eu_scr[0:hu0 - POOL_HIST, :] = jnp.zeros((hu0 - POOL_HIST, MIX_HALF), F32)
        eg_scr[0:hg0 - CONV_HIST, :] = jnp.zeros((hg0 - CONV_HIST, MIX_HALF), F32)
        eu_scr[hu0 - POOL_HIST:hu0, :] = hu_ref[0]
        eg_scr[hg0 - CONV_HIST:hg0, :] = hg_ref[0]

    @pl.when(t > 0)
    def _():
        eu_scr[0:hu0, :] = eu_scr[tm:tm + hu0, :]
        eg_scr[0:hg0, :] = eg_scr[tm:tm + hg0, :]

    u = u_ref[0]
    gt = gt_ref[0]
    glu = a_ref[0] * _sigmoid(gt)
    eu_scr[hu0:hu0 + tm, :] = u
    eg_scr[hg0:hg0 + tm, :] = glu

    pos = q_start + t * tm + lax.broadcasted_iota(jnp.int32, (tm, 1), 0)
    for gi, w in enumerate(POOL_WINDOWS):
        c0 = gi * POOL_GC
        wsum = u[:, c0:c0 + POOL_GC]
        for d in range(1, w):
            wsum = wsum + eu_scr[hu0 - d:hu0 - d + tm, c0:c0 + POOL_GC]
        cnt = jnp.minimum(w, pos + 1).astype(F32)
        pooled = wsum / cnt - u[:, c0:c0 + POOL_GC]
        po = _dot(pooled.astype(BF16), pw_ref[gi].astype(BF16))
        po_ref[0, :, c0:c0 + POOL_GC] = po * ps_ref[:, c0:c0 + POOL_GC]

    cv = jnp.broadcast_to(cb_ref[...], (tm, MIX_HALF))
    for j in range(CONV_WIDTH):
        s0 = hg0 - CONV_HIST + j
        cv = cv + cw_ref[j:j + 1, :] * eg_scr[s0:s0 + tm, :]
    mu = jnp.mean(cv, axis=-1, keepdims=True)
    dv = cv - mu
    var = jnp.mean(dv * dv, axis=-1, keepdims=True)
    y = dv * lax.rsqrt(var + EPS) * lg_ref[...] + lbias_ref[...]
    co_ref[0] = y * _sigmoid(y)

    @pl.when(t == pl.num_programs(1) - 1)
    def _():
        nu_ref[0] = eu_scr[hu0 + tm - POOL_HIST:hu0 + tm, :]
        ng_ref[0] = eg_scr[hg0 + tm - CONV_HIST:hg0 + tm, :]


def _odd(u, a, gt, hist_u, hist_g, io, pool_w, pool_scale, conv_w, conv_b, ln_g, ln_b, q_start):
    bx, tx, _ = u.shape
    tm = min(tx, 256)
    tile = pl.BlockSpec((1, tm, MIX_HALF), lambda b, t: (b, t, 0))
    vec = pl.BlockSpec((1, MIX_HALF), lambda b, t: (0, 0))
    hu_spec = pl.BlockSpec((None, 1, POOL_HIST, MIX_HALF), lambda b, t: (io if hist_u.shape[0] > 1 else 0, b, 0, 0))
    hg_spec = pl.BlockSpec((None, 1, CONV_HIST, MIX_HALF), lambda b, t: (io if hist_g.shape[0] > 1 else 0, b, 0, 0))
    return pl.pallas_call(
        functools.partial(_odd_kernel, tm=tm, q_start=q_start),
        grid=(bx, tx // tm),
        in_specs=[tile, tile, tile, hu_spec, hg_spec,
                  pl.BlockSpec((None, len(POOL_WINDOWS), POOL_GC, POOL_GC), lambda b, t: (io, 0, 0, 0)),
                  vec,
                  pl.BlockSpec((None, CONV_WIDTH, MIX_HALF), lambda b, t: (io, 0, 0)),
                  vec, vec, vec],
        out_specs=[tile, tile,
                   pl.BlockSpec((1, POOL_HIST, MIX_HALF), lambda b, t: (b, 0, 0)),
                   pl.BlockSpec((1, CONV_HIST, MIX_HALF), lambda b, t: (b, 0, 0))],
        out_shape=[jax.ShapeDtypeStruct((bx, tx, MIX_HALF), F32), jax.ShapeDtypeStruct((bx, tx, MIX_HALF), F32),
                   jax.ShapeDtypeStruct((bx, POOL_HIST, MIX_HALF), F32),
                   jax.ShapeDtypeStruct((bx, CONV_HIST, MIX_HALF), F32)],
        scratch_shapes=[pltpu.VMEM((2 * SUBLANES + tm, MIX_HALF), F32),
                        pltpu.VMEM((4 * SUBLANES + tm, MIX_HALF), F32)],
        compiler_params=_cparams("arbitrary", "arbitrary"),
        name="odd_mixer",
    )(u, a, gt, hist_u, hist_g, pool_w, pool_scale[io:io + 1], conv_w, conv_b[io:io + 1], ln_g[io:io + 1],
      ln_b[io:io + 1])


def _post_kernel(a_ref, b_ref, x_ref, g1_ref, sh2_ref, sc2_ref, g2_ref, ng_ref, woa_ref, wob_ref,
                 w1_ref, w2_ref, o_ref, x1_scr, h2_scr, acc_scr):
    c = pl.program_id(2)

    @pl.when(c == 0)
    def _():
        m = _dot(a_ref[0].astype(BF16), woa_ref[...]) + _dot(b_ref[0].astype(BF16), wob_ref[...])
        x1 = x_ref[0] + g1_ref[0] * _rms(m, ng_ref[1:2, :])
        x1_scr[...] = x1
        h2_scr[...] = (_rms(x1, ng_ref[2:3, :]) * (1.0 + sc2_ref[0]) + sh2_ref[0]).astype(BF16)
        acc_scr[...] = jnp.zeros_like(acc_scr)

    hc = jnp.maximum(_dot(h2_scr[...], w1_ref[...]), 0.0)
    acc_scr[...] += _dot((hc * hc).astype(BF16), w2_ref[...])

    @pl.when(c == pl.num_programs(2) - 1)
    def _():
        o_ref[0] = x1_scr[...] + g2_ref[0] * _rms(acc_scr[...], ng_ref[3:4, :])


def _post(a, b, x, mod, norm_g, layer, wo_bf16, wo_layer, w1_bf16, w2_bf16):
    bx, tx, _ = x.shape
    tm = min(tx, ROW_TILE)
    half = pl.BlockSpec((1, tm, MIX_HALF), lambda b_, t, c: (b_, t, 0))
    full = pl.BlockSpec((1, tm, D_MODEL), lambda b_, t, c: (b_, t, 0))
    return pl.pallas_call(
        _post_kernel,
        grid=(bx, tx // tm, D_FF // FF_TILE),
        in_specs=[half, half, full,
                  _mod_spec(mod, tm, 2), _mod_spec(mod, tm, 3), _mod_spec(mod, tm, 4), _mod_spec(mod, tm, 5),
                  pl.BlockSpec((None, 4, D_MODEL), lambda b_, t, c: (layer, 0, 0)),
                  pl.BlockSpec((None, MIX_HALF, D_MODEL), lambda b_, t, c: (wo_layer, 0, 0)),
                  pl.BlockSpec((None, MIX_HALF, D_MODEL), lambda b_, t, c: (wo_layer, 1, 0)),
                  pl.BlockSpec((None, D_MODEL, FF_TILE), lambda b_, t, c: (layer, 0, c)),
                  pl.BlockSpec((None, FF_TILE, D_MODEL), lambda b_, t, c: (layer, c, 0))],
        out_specs=full,
        out_shape=jax.ShapeDtypeStruct((bx, tx, D_MODEL), F32),
        scratch_shapes=[pltpu.VMEM((tm, D_MODEL), F32), pltpu.VMEM((tm, D_MODEL), BF16),
                        pltpu.VMEM((tm, D_MODEL), F32)],
        compiler_params=_cparams("arbitrary", "arbitrary", "arbitrary"),
        name="post",
    )(a, b, x, mod, mod, mod, mod, norm_g, wo_bf16, wo_bf16, w1_bf16, w2_bf16)


def kernel(x_prompt, x_sample, c_prompt, c_sample, cache_sb_k, cache_sb_v, state_hgrn, state_pool, state_conv,
           page_table, norm_g, ada_w, ada_b, even_w_in, even_w_out, sb_bias, hgrn_lb_logits, hgrn_norm_g,
           odd_w_in, odd_w_out, pool_w, pool_scale, conv_w, conv_b, conv_ln_g, conv_ln_b, mlp_w1, mlp_w2):
    n_even = even_w_in.shape[0]
    bp, tp, _ = x_prompt.shape
    bs, ts, _ = x_sample.shape
    n_pool, page = cache_sb_k.shape[1], cache_sb_k.shape[2]
    past_len = page_table.shape[1] * page
    half = MIX_HALF

    lb_cum = jnp.cumsum(jax.nn.softmax(hgrn_lb_logits.astype(F32), axis=0), axis=0)
    lower_bounds = jnp.maximum(lb_cum - lb_cum[:1], 0.0)

    mod_all = _ada(jnp.concatenate([c_prompt, c_sample], axis=0), ada_w, ada_b)
    even_w_in_b = even_w_in.astype(BF16)
    even_w_out_b = even_w_out.astype(BF16)
    odd_w_in_b = odd_w_in.astype(BF16)
    odd_w_out_b = odd_w_out.astype(BF16)
    w1_b = mlp_w1.astype(BF16)
    w2_b = mlp_w2.astype(BF16)
    cache_k = cache_sb_k.transpose(0, 1, 3, 4, 2).reshape(n_even, n_pool, half, page)
    cache_v = cache_sb_v.transpose(0, 1, 3, 4, 2).reshape(n_even, n_pool, half, page)
    gn = hgrn_norm_g.reshape(n_even, half)

    def run(x, mods, is_sample):
        bx, tx, _ = x.shape
        s_list, k_list, v_list, p_list, c_list = [], [], [], [], []
        for l in range(DEPTH):
            mod = mods[l]
            if l % 2 == 0:
                ie = l // 2
                if is_sample:
                    qfig, q_b, k_b, v_b = _in_proj(
                        x, mod, norm_g, l, even_w_in_b, ie,
                        [(0, 4 * half, "rows"), (4 * half, half, "rows"), (5 * half, half, "rows"),
                         (6 * half, half, "rows")])
                    o_a, s_new = _hgrn(qfig.reshape(bs, ts, 4 * half), lower_bounds[ie:ie + 1], gn[ie:ie + 1],
                                       state_hgrn, ie)
                    bias_rows = jnp.repeat(sb_bias[ie].astype(F32), ts).reshape(SB_HEADS * ts, 1)
                    k_b = k_b.reshape(bs, ts, half)
                    v_b = v_b.reshape(bs, ts, half)
                    o_b = _sb_sample(ie, page_table, bias_rows, q_b.reshape(bs, ts, half), k_b, v_b, cache_k, cache_v)
                    k_list.append(k_b.reshape(bs, ts, SB_HEADS, SB_HD))
                    v_list.append(v_b.reshape(bs, ts, SB_HEADS, SB_HD))
                else:
                    qfig, q_b, k_t, k_tb, v_t, v_tb = _in_proj(
                        x, mod, norm_g, l, even_w_in_b, ie,
                        [(0, 4 * half, "rows"), (4 * half, half, "rows_q"), (5 * half, half, "cols"),
                         (5 * half, half, "cols_bf16"), (6 * half, half, "cols"), (6 * half, half, "cols_bf16")])
                    o_a, s_new = _hgrn(qfig, lower_bounds[ie:ie + 1], gn[ie:ie + 1],
                                       jnp.zeros((1, bx, HG_HEADS, HG_D, HG_D), F32))
                    o_b = _sb_prompt(sb_bias[ie].astype(F32), q_b, k_tb, v_tb)
                    k_list.append(k_t.reshape(bx, SB_HEADS, SB_HD, tx).transpose(0, 3, 1, 2))
                    v_list.append(v_t.reshape(bx, SB_HEADS, SB_HD, tx).transpose(0, 3, 1, 2))
                mix_a, mix_b, w_out, wo_layer = o_a.reshape(bx, tx, half), o_b.reshape(bx, tx, half), even_w_out_b, ie
                s_list.append(s_new)
            else:
                io = l // 2
                u, a, gt = _in_proj(x, mod, norm_g, l, odd_w_in_b, io,
                                    [(0, half, "rows"), (half, half, "rows"), (2 * half, half, "rows")])
                if is_sample:
                    hb, ht = bs, ts
                    hist_u, hist_g, q_start = state_pool, state_conv, past_len
                else:
                    hb, ht = bx, tx
                    hist_u = jnp.zeros((1, bx, POOL_HIST, half), F32)
                    hist_g = jnp.zeros((1, bx, CONV_HIST, half), F32)
                    q_start = 0
                po, co, new_u, new_g = _odd(u.reshape(hb, ht, half), a.reshape(hb, ht, half),
                                            gt.reshape(hb, ht, half), hist_u, hist_g, io, pool_w, pool_scale,
                                            conv_w, conv_b, conv_ln_g, conv_ln_b, q_start)
                mix_a, mix_b, w_out, wo_layer = po.reshape(bx, tx, half), co.reshape(bx, tx, half), odd_w_out_b, io
                p_list.append(new_u)
                c_list.append(new_g)
            x = _post(mix_a, mix_b, x, mod, norm_g, l, w_out, wo_layer, w1_b, w2_b)
        return (x, jnp.stack(s_list), jnp.stack(k_list), jnp.stack(v_list), jnp.stack(p_list), jnp.stack(c_list))

    mods_p = [mod_all[l, :bp].reshape(bp, 1, 6 * D_MODEL) for l in range(DEPTH)]
    mods_s = [jnp.repeat(mod_all[l, bp:], ts, axis=0).reshape(1, bs * ts, 6 * D_MODEL) for l in range(DEPTH)]
    y_p, s_p, k_p, v_p, pool_p, conv_p = run(x_prompt, mods_p, False)
    y_s, s_s, k_s, v_s, pool_s, conv_s = run(x_sample.reshape(1, bs * ts, D_MODEL), mods_s, True)
    return (y_p, y_s.reshape(bs, ts, D_MODEL), k_p, v_p, k_s, v_s, s_p, s_s, pool_p, pool_s, conv_p, conv_s)
```

```python
import functools

import jax
import jax.numpy as jnp
from jax import lax
from jax.experimental import pallas as pl
from jax.experimental.pallas import tpu as pltpu

F32 = jnp.float32
BF16 = jnp.bfloat16

D_MODEL = 1024
DEPTH = 4
MIX_HALF = D_MODEL // 2
HG_HEADS = 4
HG_D = MIX_HALF // HG_HEADS
SB_HEADS = 8
SB_HD = MIX_HALF // SB_HEADS
SB_SCALE = SB_HD ** -0.5
POOL_WINDOWS = (2, 4, 8, 16)
POOL_GC = MIX_HALF // len(POOL_WINDOWS)
POOL_HIST = max(POOL_WINDOWS) - 1
CONV_WIDTH = 31
CONV_HIST = CONV_WIDTH - 1
D_FF = 4 * D_MODEL
EPS = 1e-6

LANES = 128
SUBLANES = 8
VMEM_LIMIT = 52 * 1024 * 1024

ROW_TILE = 512
POST_GROUP = 256
FF_TILE = 1024
HG_CHUNK = 128
HG_SUB = 16
HG_SAFE_DECAY = 72.0
HG_DIRECT_SUB = 16
SB_BLOCK = 256
SB_PAIRS = 2


def _cparams(*sem):
    return pltpu.CompilerParams(dimension_semantics=sem, vmem_limit_bytes=VMEM_LIMIT)


def _dot(a, b):
    return jnp.dot(a, b, preferred_element_type=F32)


def _dot_nt(a, b):
    return lax.dot_general(a, b, (((1,), (1,)), ((), ())), preferred_element_type=F32)


def _split_dot(x, m_bf16, terms):
    acc = None
    r = x
    for i in range(terms):
        p = r.astype(BF16)
        d = _dot(p, m_bf16)
        acc = d if acc is None else acc + d
        if i + 1 < terms:
            r = r - p.astype(F32)
    return acc


def _split_dot_left(m_bf16, x, terms):
    acc = None
    r = x
    for i in range(terms):
        p = r.astype(BF16)
        d = _dot(m_bf16, p)
        acc = d if acc is None else acc + d
        if i + 1 < terms:
            r = r - p.astype(F32)
    return acc


def _sigmoid(x):
    return 1.0 / (1.0 + jnp.exp(-x))


def _log_sigmoid(x):
    return jnp.minimum(x, 0.0) - jnp.log1p(jnp.exp(-jnp.abs(x)))


def _log_sigmoid_pair(z):
    ls = jnp.minimum(z, 0.0) - jnp.log(1.0 + jnp.exp(-jnp.abs(z)))
    return ls, ls - z


def _rms(x, g):
    return x * lax.rsqrt(jnp.mean(x * x, axis=-1, keepdims=True) + EPS) * g


def _ada_kernel(c_ref, w_ref, b_ref, o_ref):
    c = c_ref[...]
    s = (c * _sigmoid(c)).astype(BF16)
    o_ref[0] = _dot(s, w_ref[0].astype(BF16)) + b_ref[0]


def _ada(c_all, ada_w, ada_b):
    nb = c_all.shape[0]
    tn = 1536
    return pl.pallas_call(
        _ada_kernel,
        grid=(DEPTH, 6 * D_MODEL // tn),
        in_specs=[
            pl.BlockSpec((nb, D_MODEL), lambda l, j: (0, 0)),
            pl.BlockSpec((1, D_MODEL, tn), lambda l, j: (l, 0, j)),
            pl.BlockSpec((1, 1, tn), lambda l, j: (l, 0, j)),
        ],
        out_specs=pl.BlockSpec((1, nb, tn), lambda l, j: (l, 0, j)),
        out_shape=jax.ShapeDtypeStruct((DEPTH, nb, 6 * D_MODEL), F32),
        compiler_params=_cparams("arbitrary", "arbitrary"),
        name="ada",
    )(c_all, ada_w, ada_b.reshape(DEPTH, 1, 6 * D_MODEL))


def _mod_spec(mod, tm, chunk):
    if mod.shape[1] == 1:
        return pl.BlockSpec((1, 1, D_MODEL), lambda b, t, *_: (b, 0, chunk))
    return pl.BlockSpec((1, tm, D_MODEL), lambda b, t, *_: (b, t, chunk))


def _inproj_kernel(x_ref, sh_ref, sc_ref, g_ref, w_ref, *o_refs, plan):
    h = _rms(x_ref[0], g_ref[0:1, :]) * (1.0 + sc_ref[0]) + sh_ref[0]
    y = _dot(h.astype(BF16), w_ref[...])
    transposed = {}
    for o_ref, (c0, n, kind) in zip(o_refs, plan):
        blk = y[:, c0:c0 + n]
        if kind == "rows":
            o_ref[0] = blk
        elif kind == "rows_q":
            o_ref[0] = (blk * SB_SCALE).astype(BF16)
        else:
            if c0 not in transposed:
                transposed[c0] = blk.T
            o_ref[0] = transposed[c0].astype(o_ref.dtype)


def _in_proj(x, mod, norm_g, layer, w_bf16, w_layer, plan):
    bx, tx, _ = x.shape
    tm = min(tx, ROW_TILE)
    n = w_bf16.shape[2]
    out_specs, out_shape = [], []
    for _, wd, kind in plan:
        if kind in ("rows", "rows_q"):
            out_specs.append(pl.BlockSpec((1, tm, wd), lambda b, t: (b, t, 0)))
            out_shape.append(jax.ShapeDtypeStruct((bx, tx, wd), F32 if kind == "rows" else BF16))
        else:
            out_specs.append(pl.BlockSpec((1, wd, tm), lambda b, t: (b, 0, t)))
            out_shape.append(jax.ShapeDtypeStruct((bx, wd, tx), F32 if kind == "cols" else BF16))
    return pl.pallas_call(
        functools.partial(_inproj_kernel, plan=tuple(plan)),
        grid=(bx, tx // tm),
        in_specs=[
            pl.BlockSpec((1, tm, D_MODEL), lambda b, t: (b, t, 0)),
            _mod_spec(mod, tm, 0),
            _mod_spec(mod, tm, 1),
            pl.BlockSpec((None, 4, D_MODEL), lambda b, t: (layer, 0, 0)),
            pl.BlockSpec((None, D_MODEL, n), lambda b, t: (w_layer, 0, 0)),
        ],
        out_specs=out_specs,
        out_shape=out_shape,
        compiler_params=_cparams("arbitrary", "arbitrary"),
        name="in_proj",
    )(x, mod, mod, norm_g, w_bf16)


def _hgrn_direct(h, q_ref, i_ref, b_scr, kk_scr, oi_scr, o_scr, *, chunk):
    sub = min(chunk, HG_DIRECT_SUB)
    sl = slice(h * HG_D, (h + 1) * HG_D)
    rows = lax.broadcasted_iota(jnp.int32, (chunk, 1), 0)
    srow = lax.broadcasted_iota(jnp.int32, (sub, 1), 0)

    def sub_block(i, carry):
        r0 = pl.multiple_of(i * sub, sub)
        b_i = b_scr[pl.ds(r0, sub), sl]
        q_i = q_ref[0, pl.ds(r0, sub), sl]
        kk_i = kk_scr[pl.ds(r0, sub), sl]
        v_i = i_ref[0, pl.ds(r0, sub), sl]
        o = oi_scr[pl.ds(r0, sub), sl]
        if chunk > sub:
            anchor = b_i[0:1, :]
            qt = q_i * jnp.exp(b_i - anchor)
            kt = jnp.where(rows < r0, kk_scr[:, sl] * jnp.exp(jnp.minimum(anchor - b_scr[:, sl], 0.0)), 0.0)
            a_off = _dot_nt(qt.astype(BF16), kt.astype(BF16))
            o = o + _dot(a_off.astype(BF16), i_ref[0, :, sl].astype(BF16))
        o_diag = jnp.zeros((sub, HG_D), F32)
        for tt in range(sub):
            e = jnp.exp(jnp.minimum(b_i[tt:tt + 1, :] - b_i, 0.0))
            d = jnp.sum(q_i[tt:tt + 1, :] * e * kk_i, axis=-1, keepdims=True)
            d = jnp.where(srow <= tt, d, 0.0)
            row = jnp.sum(d * v_i, axis=0, keepdims=True)
            o_diag = jnp.where(srow == tt, row, o_diag)
        o_scr[pl.ds(r0, sub), sl] = o + o_diag
        return carry

    lax.fori_loop(0, chunk // sub, sub_block, 0)


def _hgrn_kernel(q_ref, f_ref, i_ref, g_ref, lb_ref, gn_ref, s0_ref, o_ref, sout_ref,
                 st_scr, b_scr, kk_scr, oi_scr, o_scr, *, chunk, sub, single, safe_decay):
    t = pl.program_id(1)
    if not single:
        @pl.when(t == 0)
        def _():
            for h in range(HG_HEADS):
                st_scr[h] = s0_ref[0, h].T

    lb = lb_ref[...]
    q = q_ref[0]
    fl = f_ref[0]
    v = i_ref[0]
    c = jnp.log1p(-lb) + _log_sigmoid(fl)
    a = jnp.log(lb)
    log_f = jnp.maximum(a, c) + jnp.log1p(jnp.exp(-jnp.abs(a - c)))
    kk = (1.0 - lb) * _sigmoid(-fl)
    ri = lax.broadcasted_iota(jnp.int32, (chunk, chunk), 0)
    ci = lax.broadcasted_iota(jnp.int32, (chunk, chunk), 1)
    tril = jnp.where(ri >= ci, 1.0, 0.0).astype(BF16)
    p1 = log_f.astype(BF16)
    r1 = log_f - p1.astype(F32)
    p2 = r1.astype(BF16)
    p3 = (r1 - p2.astype(F32)).astype(BF16)
    b3 = _dot(tril, jnp.concatenate([p1, p2, p3], axis=1))
    b = b3[:, 0:MIX_HALF] + b3[:, MIX_HALF:2 * MIX_HALF] + b3[:, 2 * MIX_HALF:3 * MIX_HALF]
    b_scr[...] = b
    kk_scr[...] = kk
    b_last = b[chunk - 1:chunk, :]
    qe = (q * jnp.exp(b)).astype(BF16)
    kd = kk * jnp.exp(b_last - b)
    dec = jnp.exp(b_last)
    npad = HG_D - chunk
    prow = lax.broadcasted_iota(jnp.int32, (HG_D, 1), 0)
    for h in range(HG_HEADS):
        sl = slice(h * HG_D, (h + 1) * HG_D)
        if single:
            s = s0_ref[0, h]
            oi_scr[:, sl] = _dot(qe[:, sl], s.astype(BF16))
            zpad = jnp.zeros((npad, HG_D), F32)
            m = jnp.concatenate([kd[:, sl], zpad], axis=0)
            m = jnp.where(prow == chunk, dec[:, sl], m)
            mt = m.T
            vpad = jnp.concatenate([v[:, sl], zpad], axis=0)
            sout_ref[0, h] = mt[:, chunk:chunk + 1] * s + _dot(mt.astype(BF16), vpad.astype(BF16))
        else:
            st = st_scr[h]
            oi_scr[:, sl] = _dot_nt(qe[:, sl], st.astype(BF16))
            st_scr[h] = dec[:, sl] * st + _dot(v[:, sl].T.astype(BF16), kd[:, sl].astype(BF16))

    n_sub = chunk // sub
    worst = jnp.zeros((1, MIX_HALF), F32)
    if single:
        zpad = jnp.zeros((npad, MIX_HALF), F32)
        kk_full = jnp.concatenate([kk, zpad], axis=0)
        b_full = jnp.concatenate([b, zpad], axis=0)
        v_full = jnp.concatenate([v, zpad], axis=0)
    else:
        kk_full, b_full, v_full = kk, b, v
    anchors = [b[i * sub:i * sub + 1, :] for i in range(n_sub)]
    for i in range(n_sub):
        worst = jnp.maximum(worst, anchors[i] - b[(i + 1) * sub - 1:(i + 1) * sub, :])
    anchor_rows = jnp.concatenate([jnp.broadcast_to(an, (sub, MIX_HALF)) for an in anchors], axis=0)
    qt = (q * jnp.exp(b - anchor_rows)).astype(BF16)
    kts = []
    for i, an in enumerate(anchors):
        n = kk_full.shape[0] if single else (i + 1) * sub
        kt = (kk_full[0:n] * jnp.exp(jnp.minimum(an - b_full[0:n], safe_decay))).astype(BF16)
        if n < kk_full.shape[0]:
            kt = jnp.concatenate([kt, jnp.zeros((kk_full.shape[0] - n, MIX_HALF), BF16)], axis=0)
        kts.append(kt)
    row_blk = lax.broadcasted_iota(jnp.int32, (chunk, HG_D), 0) // sub
    causal = (lax.broadcasted_iota(jnp.int32, (chunk, HG_D), 1)
              <= lax.broadcasted_iota(jnp.int32, (chunk, HG_D), 0))
    zero_q = jnp.zeros((chunk, HG_D), BF16)
    atts = []
    for h in range(HG_HEADS):
        sl = slice(h * HG_D, (h + 1) * HG_D)
        q_cat = jnp.concatenate([jnp.where(row_blk == i, qt[:, sl], zero_q) for i in range(n_sub)], axis=1)
        k_cat = jnp.concatenate([kt[:, sl] for kt in kts], axis=1)
        atts.append(jnp.where(causal, _dot_nt(q_cat, k_cat), 0.0).astype(BF16))
    bd = (lax.broadcasted_iota(jnp.int32, (HG_HEADS * HG_D, MIX_HALF), 0) // HG_D
          == lax.broadcasted_iota(jnp.int32, (HG_HEADS * HG_D, MIX_HALF), 1) // HG_D)
    v_bd = jnp.where(bd, jnp.concatenate([v_full] * HG_HEADS, axis=0), 0.0).astype(BF16)
    o_scr[...] = oi_scr[...] + _dot(jnp.concatenate(atts, axis=1), v_bd)

    @pl.when(jnp.max(worst) > safe_decay)
    def _():
        for h in range(HG_HEADS):
            _hgrn_direct(h, q_ref, i_ref, b_scr, kk_scr, oi_scr, o_scr, chunk=chunk)

    g = g_ref[0]
    gn = gn_ref[...]
    for h in range(HG_HEADS):
        sl = slice(h * HG_D, (h + 1) * HG_D)
        gh = g[:, sl]
        o_ref[0, :, sl] = _rms(o_scr[:, sl], gn[:, sl]) * (gh * _sigmoid(gh))

    if not single:
        @pl.when(t == pl.num_programs(1) - 1)
        def _():
            for h in range(HG_HEADS):
                sout_ref[0, h] = st_scr[h].T


def _hgrn(qfig, lb_l, gn_l, s0, s0_layer=0, safe_decay=HG_SAFE_DECAY):
    bx, tx, _ = qfig.shape
    chunk = min(tx, HG_CHUNK)
    sub = min(chunk, HG_SUB)
    single = tx == chunk
    assert not single or chunk < HG_D

    def col(k):
        return pl.BlockSpec((1, chunk, MIX_HALF), lambda b, t: (b, t, k))

    vec = pl.BlockSpec((1, MIX_HALF), lambda b, t: (0, 0))
    state = pl.BlockSpec((1, HG_HEADS, HG_D, HG_D), lambda b, t: (b, 0, 0, 0))
    state_in = pl.BlockSpec((None, 1, HG_HEADS, HG_D, HG_D), lambda b, t: (s0_layer, b, 0, 0, 0))
    return pl.pallas_call(
        functools.partial(_hgrn_kernel, chunk=chunk, sub=sub, single=single, safe_decay=safe_decay),
        grid=(bx, tx // chunk),
        in_specs=[col(0), col(1), col(2), col(3), vec, vec, state_in],
        out_specs=[pl.BlockSpec((1, chunk, MIX_HALF), lambda b, t: (b, t, 0)), state],
        out_shape=[jax.ShapeDtypeStruct((bx, tx, MIX_HALF), F32),
                   jax.ShapeDtypeStruct((bx, HG_HEADS, HG_D, HG_D), F32)],
        scratch_shapes=[pltpu.VMEM((HG_HEADS, HG_D, HG_D), F32)] + [pltpu.VMEM((chunk, MIX_HALF), F32)] * 4,
        compiler_params=_cparams("arbitrary", "arbitrary"),
        name="hgrn",
    )(qfig, qfig, qfig, qfig, lb_l, gn_l, s0)


def _sb_prompt_kernel(bias_ref, q_ref, kt_ref, vt_ref, o_ref, acc_scr, carry_scr, *, blk):
    g = pl.program_id(1)
    qi = pl.program_id(2)
    lane = lax.broadcasted_iota(jnp.int32, (1, LANES), 1)
    ri = lax.broadcasted_iota(jnp.int32, (blk, blk), 0)
    ci = lax.broadcasted_iota(jnp.int32, (blk, blk), 1)
    upper = jnp.where(ri > ci, 1.0, 0.0).astype(BF16)
    ones = jnp.ones((blk, LANES), BF16)
    tri = ci < ri
    causal = jnp.concatenate([tri, tri], axis=0)
    second = lax.broadcasted_iota(jnp.int32, (2 * blk, 1), 0) >= blk
    qs, bias = [], []
    for p in range(SB_PAIRS):
        q = q_ref[0, :, p * LANES:(p + 1) * LANES]
        zero = jnp.zeros_like(q)
        qs.append(jnp.concatenate([jnp.where(lane < SB_HD, q, zero), jnp.where(lane >= SB_HD, q, zero)], axis=0))
        h0 = 2 * (g * SB_PAIRS + p)
        bias.append(jnp.where(second, bias_ref[h0 + 1], bias_ref[h0]))
    acc_scr[...] = jnp.zeros_like(acc_scr)
    carry_scr[...] = jnp.zeros_like(carry_scr)

    def step(kbs, masked):
        for p in range(SB_PAIRS):
            carry = carry_scr[p]
            pv = None
            for kb in kbs:
                c0 = pl.multiple_of(kb * blk, blk)
                kblk = kt_ref[0, p * LANES:(p + 1) * LANES, pl.ds(c0, blk)]
                vblk = vt_ref[0, p * LANES:(p + 1) * LANES, pl.ds(c0, blk)]
                ls, lk = _log_sigmoid_pair(_dot(qs[p], kblk) + bias[p])
                if masked:
                    lk = jnp.where(causal, lk, 0.0)
                lkb = lk.astype(BF16)
                w = jnp.exp(ls + _dot(lkb, upper) + jnp.concatenate([carry] * (blk // LANES), axis=1))
                if masked:
                    w = jnp.where(causal, w, 0.0)
                pv_k = _dot_nt(w.astype(BF16), vblk)
                pv = pv_k if pv is None else pv + pv_k
                carry = carry + _dot(lkb, ones)
            acc_scr[p] += pv
            carry_scr[p] = carry

    step([qi], True)

    def body(j, c):
        kb = qi - 1 - 2 * j
        step([kb, kb - 1], False)
        return c

    lax.fori_loop(0, qi // 2, body, 0)

    @pl.when(qi % 2 == 1)
    def _():
        step([0], False)
    for p in range(SB_PAIRS):
        o_ref[0, :, p * LANES:(p + 1) * LANES] = jnp.where(lane < SB_HD, acc_scr[p, 0:blk], acc_scr[p, blk:2 * blk])


def _sb_prompt(bias_l, q_bf16, kt_bf16, vt_bf16):
    bx, tx, _ = q_bf16.shape
    blk = SB_BLOCK
    wd = SB_PAIRS * LANES
    rows = pl.BlockSpec((1, blk, wd), lambda b, g, qi: (b, qi, g))
    full = pl.BlockSpec((1, wd, tx), lambda b, g, qi: (b, g, 0))
    return pl.pallas_call(
        functools.partial(_sb_prompt_kernel, blk=blk),
        grid=(bx, MIX_HALF // wd, tx // blk),
        in_specs=[pl.BlockSpec(memory_space=pltpu.SMEM), rows, full, full],
        out_specs=rows,
        out_shape=jax.ShapeDtypeStruct((bx, tx, MIX_HALF), F32),
        scratch_shapes=[pltpu.VMEM((SB_PAIRS, 2 * blk, LANES), F32), pltpu.VMEM((SB_PAIRS, 2 * blk, LANES), F32)],
        compiler_params=_cparams("arbitrary", "arbitrary", "arbitrary"),
        name="sb_prompt",
    )(bias_l, q_bf16, kt_bf16, vt_bf16)


def _sb_sample_kernel(pt_ref, q_ref, kn_ref, vn_ref, bias_ref, *rest, tq, page, n_pages):
    kc_refs, vc_refs, o_ref = rest[:n_pages], rest[n_pages:2 * n_pages], rest[2 * n_pages]
    nrow = SB_HEADS * tq
    rowh = lax.broadcasted_iota(jnp.int32, (nrow, MIX_HALF), 0) // tq
    colh = lax.broadcasted_iota(jnp.int32, (nrow, MIX_HALF), 1) // SB_HD
    own = rowh == colh
    bias = bias_ref[...]
    ri = lax.broadcasted_iota(jnp.int32, (page, page), 0)
    ci = lax.broadcasted_iota(jnp.int32, (page, page), 1)
    upper = jnp.where(ri > ci, 1.0, 0.0).astype(BF16)
    qbd = (jnp.where(own, jnp.concatenate([q_ref[0]] * SB_HEADS, axis=0), 0.0) * SB_SCALE).astype(BF16)
    pad = jnp.zeros((page - tq, MIX_HALF), F32)
    kn = jnp.concatenate([kn_ref[0], pad], axis=0).astype(BF16)
    vn = jnp.concatenate([vn_ref[0], pad], axis=0).astype(BF16)
    qpos = lax.broadcasted_iota(jnp.int32, (nrow, page), 0) % tq
    kpos = lax.broadcasted_iota(jnp.int32, (nrow, page), 1)
    new_mask = kpos < qpos

    ls0, lk0 = _log_sigmoid_pair(_dot_nt(qbd, kn) + bias)
    ls_all, lk_all = [ls0], [jnp.where(new_mask, lk0, 0.0)]
    for j in range(n_pages):
        ls, lk = _log_sigmoid_pair(_dot(qbd, kc_refs[j][0, 0].astype(BF16)) + bias)
        ls_all.append(ls)
        lk_all.append(lk)
    lk_cat = jnp.concatenate(lk_all, axis=0)
    suffix = _split_dot(lk_cat, upper, 2)
    total = jnp.sum(lk_cat, axis=-1, keepdims=True)
    carry = jnp.zeros((nrow, 1), F32)
    acc = jnp.zeros((nrow, MIX_HALF), F32)
    for j in range(n_pages + 1):
        r = slice(j * nrow, (j + 1) * nrow)
        w = jnp.exp(ls_all[j] + suffix[r] + carry)
        if j == 0:
            acc = acc + _dot(jnp.where(new_mask, w, 0.0).astype(BF16), vn)
        else:
            acc = acc + _dot_nt(w.astype(BF16), vc_refs[j - 1][0, 0].astype(BF16))
        carry = carry + total[r]
    masked = jnp.where(own, acc, 0.0)
    out = masked[0:tq, :]
    for h in range(1, SB_HEADS):
        out = out + masked[h * tq:(h + 1) * tq, :]
    o_ref[0] = out


def _sb_sample(layer, page_table, bias_rows, q, k_new, v_new, cache_k, cache_v):
    bx, tq, _ = q.shape
    n_pages = page_table.shape[1]
    page = cache_k.shape[3]
    nrow = SB_HEADS * tq
    seq = pl.BlockSpec((1, tq, MIX_HALF), lambda b, pt: (b, 0, 0))

    def cache(j):
        return pl.BlockSpec((1, 1, MIX_HALF, page),
                            lambda b, pt: (layer, pt[b * n_pages + (n_pages - 1 - j)], 0, 0))

    pages = [cache(j) for j in range(n_pages)]
    return pl.pallas_call(
        functools.partial(_sb_sample_kernel, tq=tq, page=page, n_pages=n_pages),
        grid_spec=pltpu.PrefetchScalarGridSpec(
            num_scalar_prefetch=1,
            grid=(bx,),
            in_specs=[seq, seq, seq, pl.BlockSpec((nrow, 1), lambda b, pt: (0, 0))] + pages + pages,
            out_specs=seq,
        ),
        out_shape=jax.ShapeDtypeStruct((bx, tq, MIX_HALF), F32),
        compiler_params=_cparams("arbitrary"),
        name="sb_sample",
    )(page_table.reshape(-1), q, k_new, v_new, bias_rows, *([cache_k] * n_pages), *([cache_v] * n_pages))


def _odd_kernel(u_ref, a_ref, gt_ref, hu_ref, hg_ref, pw_ref, ps_ref, cw_ref, cb_ref, lg_ref, lbias_ref,
                po_ref, co_ref, nu_ref, ng_ref, eu_scr, eg_scr, sh_scr, *, tm, q_start):
    t = pl.program_id(1)
    hu0 = 2 * SUBLANES
    hg0 = 4 * SUBLANES

    @pl.when(t == 0)
    def _():
        eu_scr[0:hu0 - POOL_HIST, :] = jnp.zeros((hu0 - POOL_HIST, MIX_HALF), F32)
        eg_scr[0:hg0 - CONV_HIST, :] = jnp.zeros((hg0 - CONV_HIST, MIX_HALF), F32)
        eu_scr[hu0 - POOL_HIST:hu0, :] = hu_ref[0]
        eg_scr[hg0 - CONV_HIST:hg0, :] = hg_ref[0]

    @pl.when(t > 0)
    def _():
        eu_scr[0:hu0, :] = eu_scr[tm:tm + hu0, :]
        eg_scr[0:hg0, :] = eg_scr[tm:tm + hg0, :]

    u = u_ref[0]
    gt = gt_ref[0]
    glu = a_ref[0] * _sigmoid(gt)
    eu_scr[hu0:hu0 + tm, :] = u
    eg_scr[hg0:hg0 + tm, :] = glu

    pos = q_start + t * tm + lax.broadcasted_iota(jnp.int32, (tm, 1), 0)
    for gi, w in enumerate(POOL_WINDOWS):
        c0 = gi * POOL_GC
        wsum = u[:, c0:c0 + POOL_GC]
        for d in range(1, w):
            wsum = wsum + eu_scr[hu0 - d:hu0 - d + tm, c0:c0 + POOL_GC]
        cnt = jnp.minimum(w, pos + 1).astype(F32)
        pooled = wsum / cnt - u[:, c0:c0 + POOL_GC]
        po = _dot(pooled.astype(BF16), pw_ref[gi].astype(BF16))
        po_ref[0, :, c0:c0 + POOL_GC] = po * ps_ref[:, c0:c0 + POOL_GC]

    cv = jnp.broadcast_to(cb_ref[...], (tm, MIX_HALF))
    first = hg0 - CONV_HIST
    for r in range(SUBLANES):
        taps = [j for j in range(CONV_WIDTH) if (first + j) % SUBLANES == r]
        if not taps:
            continue
        span = (first + taps[-1]) // SUBLANES * SUBLANES
        if r == 0:
            src = eg_scr
        else:
            sh_scr[0:span + tm, :] = eg_scr[r:r + span + tm, :]
            src = sh_scr
        for j in taps:
            a0 = first + j - r
            cv = cv + cw_ref[j:j + 1, :] * src[a0:a0 + tm, :]
    mu = jnp.mean(cv, axis=-1, keepdims=True)
    dv = cv - mu
    var = jnp.mean(dv * dv, axis=-1, keepdims=True)
    y = dv * lax.rsqrt(var + EPS) * lg_ref[...] + lbias_ref[...]
    co_ref[0] = y * _sigmoid(y)

    @pl.when(t == pl.num_programs(1) - 1)
    def _():
        nu_ref[0] = eu_scr[hu0 + tm - POOL_HIST:hu0 + tm, :]
        ng_ref[0] = eg_scr[hg0 + tm - CONV_HIST:hg0 + tm, :]


def _odd(u, a, gt, hist_u, hist_g, io, pool_w, pool_scale, conv_w, conv_b, ln_g, ln_b, q_start):
    bx, tx, _ = u.shape
    tm = min(tx, 256)
    tile = pl.BlockSpec((1, tm, MIX_HALF), lambda b, t: (b, t, 0))
    vec = pl.BlockSpec((1, MIX_HALF), lambda b, t: (0, 0))
    hu_spec = pl.BlockSpec((None, 1, POOL_HIST, MIX_HALF), lambda b, t: (io if hist_u.shape[0] > 1 else 0, b, 0, 0))
    hg_spec = pl.BlockSpec((None, 1, CONV_HIST, MIX_HALF), lambda b, t: (io if hist_g.shape[0] > 1 else 0, b, 0, 0))
    return pl.pallas_call(
        functools.partial(_odd_kernel, tm=tm, q_start=q_start),
        grid=(bx, tx // tm),
        in_specs=[tile, tile, tile, hu_spec, hg_spec,
                  pl.BlockSpec((None, len(POOL_WINDOWS), POOL_GC, POOL_GC), lambda b, t: (io, 0, 0, 0)),
                  vec,
                  pl.BlockSpec((None, CONV_WIDTH, MIX_HALF), lambda b, t: (io, 0, 0)),
                  vec, vec, vec],
        out_specs=[tile, tile,
                   pl.BlockSpec((1, POOL_HIST, MIX_HALF), lambda b, t: (b, 0, 0)),
                   pl.BlockSpec((1, CONV_HIST, MIX_HALF), lambda b, t: (b, 0, 0))],
        out_shape=[jax.ShapeDtypeStruct((bx, tx, MIX_HALF), F32), jax.ShapeDtypeStruct((bx, tx, MIX_HALF), F32),
                   jax.ShapeDtypeStruct((bx, POOL_HIST, MIX_HALF), F32),
                   jax.ShapeDtypeStruct((bx, CONV_HIST, MIX_HALF), F32)],
        scratch_shapes=[pltpu.VMEM((2 * SUBLANES + tm, MIX_HALF), F32),
                        pltpu.VMEM((4 * SUBLANES + tm, MIX_HALF), F32),
                        pltpu.VMEM((4 * SUBLANES + tm, MIX_HALF), F32)],
        compiler_params=_cparams("arbitrary", "arbitrary"),
        name="odd_mixer",
    )(u, a, gt, hist_u, hist_g, pool_w, pool_scale[io:io + 1], conv_w, conv_b[io:io + 1], ln_g[io:io + 1],
      ln_b[io:io + 1])


def _post_kernel(a_ref, b_ref, x_ref, g1_ref, sh2_ref, sc2_ref, g2_ref, ng_ref, woa_ref, wob_ref,
                 w1_ref, w2_ref, o_ref, *, tm):
    def rows_of(ref, rows):
        return ref[0] if ref.shape[1] == 1 else ref[0, rows, :]

    for r0 in range(0, tm, POST_GROUP):
        rows = slice(r0, r0 + POST_GROUP)
        m = _dot(a_ref[0, rows, :].astype(BF16), woa_ref[...]) + _dot(b_ref[0, rows, :].astype(BF16), wob_ref[...])
        x1 = x_ref[0, rows, :] + rows_of(g1_ref, rows) * _rms(m, ng_ref[1:2, :])
        h2 = (_rms(x1, ng_ref[2:3, :]) * (1.0 + rows_of(sc2_ref, rows)) + rows_of(sh2_ref, rows)).astype(BF16)
        acc = None
        for c0 in range(0, D_FF, FF_TILE):
            hc = jnp.maximum(_dot(h2, w1_ref[:, c0:c0 + FF_TILE]), 0.0)
            d = _dot((hc * hc).astype(BF16), w2_ref[c0:c0 + FF_TILE, :])
            acc = d if acc is None else acc + d
        o_ref[0, rows, :] = x1 + rows_of(g2_ref, rows) * _rms(acc, ng_ref[3:4, :])


def _post(a, b, x, mod, norm_g, layer, wo_bf16, wo_layer, w1_bf16, w2_bf16):
    bx, tx, _ = x.shape
    tm = min(tx, ROW_TILE)
    half = pl.BlockSpec((1, tm, MIX_HALF), lambda b_, t: (b_, t, 0))
    full = pl.BlockSpec((1, tm, D_MODEL), lambda b_, t: (b_, t, 0))
    once = pl.Buffered(1)
    return pl.pallas_call(
        functools.partial(_post_kernel, tm=tm),
        grid=(bx, tx // tm),
        in_specs=[half, half, full,
                  _mod_spec(mod, tm, 2), _mod_spec(mod, tm, 3), _mod_spec(mod, tm, 4), _mod_spec(mod, tm, 5),
                  pl.BlockSpec((None, 4, D_MODEL), lambda b_, t: (layer, 0, 0)),
                  pl.BlockSpec((None, MIX_HALF, D_MODEL), lambda b_, t: (wo_layer, 0, 0), pipeline_mode=once),
                  pl.BlockSpec((None, MIX_HALF, D_MODEL), lambda b_, t: (wo_layer, 1, 0), pipeline_mode=once),
                  pl.BlockSpec((None, D_MODEL, D_FF), lambda b_, t: (layer, 0, 0), pipeline_mode=once),
                  pl.BlockSpec((None, D_FF, D_MODEL), lambda b_, t: (layer, 0, 0), pipeline_mode=once)],
        out_specs=full,
        out_shape=jax.ShapeDtypeStruct((bx, tx, D_MODEL), F32),
        compiler_params=_cparams("arbitrary", "arbitrary"),
        name="post",
    )(a, b, x, mod, mod, mod, mod, norm_g, wo_bf16, wo_bf16, w1_bf16, w2_bf16)


def kernel(x_prompt, x_sample, c_prompt, c_sample, cache_sb_k, cache_sb_v, state_hgrn, state_pool, state_conv,
           page_table, norm_g, ada_w, ada_b, even_w_in, even_w_out, sb_bias, hgrn_lb_logits, hgrn_norm_g,
           odd_w_in, odd_w_out, pool_w, pool_scale, conv_w, conv_b, conv_ln_g, conv_ln_b, mlp_w1, mlp_w2):
    n_even = even_w_in.shape[0]
    bp, tp, _ = x_prompt.shape
    bs, ts, _ = x_sample.shape
    n_pool, page = cache_sb_k.shape[1], cache_sb_k.shape[2]
    past_len = page_table.shape[1] * page
    half = MIX_HALF

    lb_cum = jnp.cumsum(jax.nn.softmax(hgrn_lb_logits.astype(F32), axis=0), axis=0)
    lower_bounds = jnp.maximum(lb_cum - lb_cum[:1], 0.0)

    mod_all = _ada(jnp.concatenate([c_prompt, c_sample], axis=0), ada_w, ada_b)
    even_w_in_b = even_w_in.astype(BF16)
    even_w_out_b = even_w_out.astype(BF16)
    odd_w_in_b = odd_w_in.astype(BF16)
    odd_w_out_b = odd_w_out.astype(BF16)
    w1_b = mlp_w1.astype(BF16)
    w2_b = mlp_w2.astype(BF16)
    cache_k = cache_sb_k.transpose(0, 1, 3, 4, 2).reshape(n_even, n_pool, half, page)
    cache_v = cache_sb_v.transpose(0, 1, 3, 4, 2).reshape(n_even, n_pool, half, page)
    gn = hgrn_norm_g.reshape(n_even, half)

    def run(x, mods, is_sample):
        bx, tx, _ = x.shape
        s_list, k_list, v_list, p_list, c_list = [], [], [], [], []
        for l in range(DEPTH):
            mod = mods[l]
            if l % 2 == 0:
                ie = l // 2
                if is_sample:
                    qfig, q_b, k_b, v_b = _in_proj(
                        x, mod, norm_g, l, even_w_in_b, ie,
                        [(0, 4 * half, "rows"), (4 * half, half, "rows"), (5 * half, half, "rows"),
                         (6 * half, half, "rows")])
                    o_a, s_new = _hgrn(qfig.reshape(bs, ts, 4 * half), lower_bounds[ie:ie + 1], gn[ie:ie + 1],
                                       state_hgrn, ie)
                    bias_rows = jnp.repeat(sb_bias[ie].astype(F32), ts).reshape(SB_HEADS * ts, 1)
                    k_b = k_b.reshape(bs, ts, half)
                    v_b = v_b.reshape(bs, ts, half)
                    o_b = _sb_sample(ie, page_table, bias_rows, q_b.reshape(bs, ts, half), k_b, v_b, cache_k, cache_v)
                    k_list.append(k_b.reshape(bs, ts, SB_HEADS, SB_HD))
                    v_list.append(v_b.reshape(bs, ts, SB_HEADS, SB_HD))
                else:
                    qfig, q_b, k_t, k_tb, v_t, v_tb = _in_proj(
                        x, mod, norm_g, l, even_w_in_b, ie,
                        [(0, 4 * half, "rows"), (4 * half, half, "rows_q"), (5 * half, half, "cols"),
                         (5 * half, half, "cols_bf16"), (6 * half, half, "cols"), (6 * half, half, "cols_bf16")])
                    o_a, s_new = _hgrn(qfig, lower_bounds[ie:ie + 1], gn[ie:ie + 1],
                                       jnp.zeros((1, bx, HG_HEADS, HG_D, HG_D), F32))
                    o_b = _sb_prompt(sb_bias[ie].astype(F32), q_b, k_tb, v_tb)
                    k_list.append(k_t.reshape(bx, SB_HEADS, SB_HD, tx).transpose(0, 3, 1, 2))
                    v_list.append(v_t.reshape(bx, SB_HEADS, SB_HD, tx).transpose(0, 3, 1, 2))
                mix_a, mix_b, w_out, wo_layer = o_a.reshape(bx, tx, half), o_b.reshape(bx, tx, half), even_w_out_b, ie
                s_list.append(s_new)
            else:
                io = l // 2
                u, a, gt = _in_proj(x, mod, norm_g, l, odd_w_in_b, io,
                                    [(0, half, "rows"), (half, half, "rows"), (2 * half, half, "rows")])
                if is_sample:
                    hb, ht = bs, ts
                    hist_u, hist_g, q_start = state_pool, state_conv, past_len
                else:
                    hb, ht = bx, tx
                    hist_u = jnp.zeros((1, bx, POOL_HIST, half), F32)
                    hist_g = jnp.zeros((1, bx, CONV_HIST, half), F32)
                    q_start = 0
                po, co, new_u, new_g = _odd(u.reshape(hb, ht, half), a.reshape(hb, ht, half),
                                            gt.reshape(hb, ht, half), hist_u, hist_g, io, pool_w, pool_scale,
                                            conv_w, conv_b, conv_ln_g, conv_ln_b, q_start)
                mix_a, mix_b, w_out, wo_layer = po.reshape(bx, tx, half), co.reshape(bx, tx, half), odd_w_out_b, io
                p_list.append(new_u)
                c_list.append(new_g)
            x = _post(mix_a, mix_b, x, mod, norm_g, l, w_out, wo_layer, w1_b, w2_b)
        return (x, jnp.stack(s_list), jnp.stack(k_list), jnp.stack(v_list), jnp.stack(p_list), jnp.stack(c_list))

    mods_p = [mod_all[l, :bp].reshape(bp, 1, 6 * D_MODEL) for l in range(DEPTH)]
    mods_s = [jnp.repeat(mod_all[l, bp:], ts, axis=0).reshape(1, bs * ts, 6 * D_MODEL) for l in range(DEPTH)]
    y_p, s_p, k_p, v_p, pool_p, conv_p = run(x_prompt, mods_p, False)
    y_s, s_s, k_s, v_s, pool_s, conv_s = run(x_sample.reshape(1, bs * ts, D_MODEL), mods_s, True)
    return (y_p, y_s.reshape(bs, ts, D_MODEL), k_p, v_p, k_s, v_s, s_p, s_s, pool_p, pool_s, conv_p, conv_s)
```

```python
import functools

import jax
import jax.numpy as jnp
from jax import lax
from jax.experimental import pallas as pl
from jax.experimental.pallas import tpu as pltpu

F32 = jnp.float32
BF16 = jnp.bfloat16

D_MODEL = 1024
DEPTH = 4
MIX_HALF = D_MODEL // 2
HG_HEADS = 4
HG_D = MIX_HALF // HG_HEADS
SB_HEADS = 8
SB_HD = MIX_HALF // SB_HEADS
SB_SCALE = SB_HD ** -0.5
POOL_WINDOWS = (2, 4, 8, 16)
POOL_GC = MIX_HALF // len(POOL_WINDOWS)
POOL_HIST = max(POOL_WINDOWS) - 1
CONV_WIDTH = 31
CONV_HIST = CONV_WIDTH - 1
D_FF = 4 * D_MODEL
EPS = 1e-6

LANES = 128
SUBLANES = 8
VMEM_LIMIT = 52 * 1024 * 1024

ROW_TILE = 512
POST_GROUP = 256
FF_TILE = 1024
HG_CHUNK = 128
HG_SUB = 16
HG_SAFE_DECAY = 72.0
HG_DIRECT_SUB = 16
SB_BLOCK = 256
SB_PAIRS = 2


def _cparams(*sem):
    return pltpu.CompilerParams(dimension_semantics=sem, vmem_limit_bytes=VMEM_LIMIT)


def _dot(a, b):
    return jnp.dot(a, b, preferred_element_type=F32)


def _dot_nt(a, b):
    return lax.dot_general(a, b, (((1,), (1,)), ((), ())), preferred_element_type=F32)


def _split_dot(x, m_bf16, terms):
    acc = None
    r = x
    for i in range(terms):
        p = r.astype(BF16)
        d = _dot(p, m_bf16)
        acc = d if acc is None else acc + d
        if i + 1 < terms:
            r = r - p.astype(F32)
    return acc


def _split_dot_left(m_bf16, x, terms):
    acc = None
    r = x
    for i in range(terms):
        p = r.astype(BF16)
        d = _dot(m_bf16, p)
        acc = d if acc is None else acc + d
        if i + 1 < terms:
            r = r - p.astype(F32)
    return acc


def _sigmoid(x):
    return 1.0 / (1.0 + jnp.exp(-x))


def _log_sigmoid(x):
    return jnp.minimum(x, 0.0) - jnp.log1p(jnp.exp(-jnp.abs(x)))


def _log_sigmoid_pair(z):
    ls = jnp.minimum(z, 0.0) - jnp.log(1.0 + jnp.exp(-jnp.abs(z)))
    return ls, ls - z


def _rms(x, g):
    return x * lax.rsqrt(jnp.mean(x * x, axis=-1, keepdims=True) + EPS) * g


def _ada_kernel(c_ref, w_ref, b_ref, o_ref):
    c = c_ref[...]
    s = (c * _sigmoid(c)).astype(BF16)
    o_ref[0] = _dot(s, w_ref[0].astype(BF16)) + b_ref[0]


def _ada(c_all, ada_w, ada_b):
    nb = c_all.shape[0]
    tn = 1536
    return pl.pallas_call(
        _ada_kernel,
        grid=(DEPTH, 6 * D_MODEL // tn),
        in_specs=[
            pl.BlockSpec((nb, D_MODEL), lambda l, j: (0, 0)),
            pl.BlockSpec((1, D_MODEL, tn), lambda l, j: (l, 0, j)),
            pl.BlockSpec((1, 1, tn), lambda l, j: (l, 0, j)),
        ],
        out_specs=pl.BlockSpec((1, nb, tn), lambda l, j: (l, 0, j)),
        out_shape=jax.ShapeDtypeStruct((DEPTH, nb, 6 * D_MODEL), F32),
        compiler_params=_cparams("arbitrary", "arbitrary"),
        name="ada",
    )(c_all, ada_w, ada_b.reshape(DEPTH, 1, 6 * D_MODEL))


def _mod_spec(mod, tm, chunk):
    if mod.shape[1] == 1:
        return pl.BlockSpec((1, 1, D_MODEL), lambda b, t, *_: (b, 0, chunk))
    return pl.BlockSpec((1, tm, D_MODEL), lambda b, t, *_: (b, t, chunk))


def _inproj_kernel(x_ref, sh_ref, sc_ref, g_ref, w_ref, *rest, plan):
    n_prev = sum(kind == "cols_after" for _, _, kind in plan)
    prev_refs, o_refs = list(rest[:n_prev]), rest[n_prev:]
    h = _rms(x_ref[0], g_ref[0:1, :]) * (1.0 + sc_ref[0]) + sh_ref[0]
    y = _dot(h.astype(BF16), w_ref[...])
    transposed = {}
    for o_ref, (c0, n, kind) in zip(o_refs, plan):
        blk = y[:, c0:c0 + n]
        if kind == "rows":
            o_ref[0] = blk
        elif kind == "rows_q":
            o_ref[0] = (blk * SB_SCALE).astype(BF16)
        else:
            if c0 not in transposed:
                transposed[c0] = blk.T
            if kind == "cols_after":
                o_ref[0, 0] = prev_refs.pop(0)[0]
                o_ref[1, 0] = transposed[c0]
            else:
                o_ref[0] = transposed[c0].astype(o_ref.dtype)


def _in_proj(x, mod, norm_g, layer, w_bf16, w_layer, plan, prev=()):
    bx, tx, _ = x.shape
    tm = min(tx, ROW_TILE)
    n = w_bf16.shape[2]
    out_specs, out_shape, prev_specs = [], [], []
    for _, wd, kind in plan:
        if kind in ("rows", "rows_q"):
            out_specs.append(pl.BlockSpec((1, tm, wd), lambda b, t: (b, t, 0)))
            out_shape.append(jax.ShapeDtypeStruct((bx, tx, wd), F32 if kind == "rows" else BF16))
        elif kind == "cols_after":
            prev_specs.append(pl.BlockSpec((1, wd, tm), lambda b, t: (b, 0, t)))
            out_specs.append(pl.BlockSpec((2, 1, wd, tm), lambda b, t: (0, b, 0, t)))
            out_shape.append(jax.ShapeDtypeStruct((2, bx, wd, tx), F32))
        else:
            out_specs.append(pl.BlockSpec((1, wd, tm), lambda b, t: (b, 0, t)))
            out_shape.append(jax.ShapeDtypeStruct((bx, wd, tx), F32 if kind == "cols" else BF16))
    assert len(prev_specs) == len(prev)
    return pl.pallas_call(
        functools.partial(_inproj_kernel, plan=tuple(plan)),
        grid=(bx, tx // tm),
        in_specs=[
            pl.BlockSpec((1, tm, D_MODEL), lambda b, t: (b, t, 0)),
            _mod_spec(mod, tm, 0),
            _mod_spec(mod, tm, 1),
            pl.BlockSpec((None, 4, D_MODEL), lambda b, t: (layer, 0, 0)),
            pl.BlockSpec((None, D_MODEL, n), lambda b, t: (w_layer, 0, 0)),
        ] + prev_specs,
        out_specs=out_specs,
        out_shape=out_shape,
        compiler_params=_cparams("arbitrary", "arbitrary"),
        name="in_proj",
    )(x, mod, mod, norm_g, w_bf16, *prev)


def _hgrn_direct(h, q_ref, i_ref, b_scr, kk_scr, oi_scr, o_scr, *, chunk):
    sub = min(chunk, HG_DIRECT_SUB)
    sl = slice(h * HG_D, (h + 1) * HG_D)
    rows = lax.broadcasted_iota(jnp.int32, (chunk, 1), 0)
    srow = lax.broadcasted_iota(jnp.int32, (sub, 1), 0)

    def sub_block(i, carry):
        r0 = pl.multiple_of(i * sub, sub)
        b_i = b_scr[pl.ds(r0, sub), sl]
        q_i = q_ref[0, pl.ds(r0, sub), sl]
        kk_i = kk_scr[pl.ds(r0, sub), sl]
        v_i = i_ref[0, pl.ds(r0, sub), sl]
        o = oi_scr[pl.ds(r0, sub), sl]
        if chunk > sub:
            anchor = b_i[0:1, :]
            qt = q_i * jnp.exp(b_i - anchor)
            kt = jnp.where(rows < r0, kk_scr[:, sl] * jnp.exp(jnp.minimum(anchor - b_scr[:, sl], 0.0)), 0.0)
            a_off = _dot_nt(qt.astype(BF16), kt.astype(BF16))
            o = o + _dot(a_off.astype(BF16), i_ref[0, :, sl].astype(BF16))
        o_diag = jnp.zeros((sub, HG_D), F32)
        for tt in range(sub):
            e = jnp.exp(jnp.minimum(b_i[tt:tt + 1, :] - b_i, 0.0))
            d = jnp.sum(q_i[tt:tt + 1, :] * e * kk_i, axis=-1, keepdims=True)
            d = jnp.where(srow <= tt, d, 0.0)
            row = jnp.sum(d * v_i, axis=0, keepdims=True)
            o_diag = jnp.where(srow == tt, row, o_diag)
        o_scr[pl.ds(r0, sub), sl] = o + o_diag
        return carry

    lax.fori_loop(0, chunk // sub, sub_block, 0)


def _hgrn_kernel(q_ref, f_ref, i_ref, g_ref, lb_ref, gn_ref, s0_ref, *rest, chunk, sub, single, stacked, safe_decay):
    if stacked:
        sprev_ref, o_ref, sout2_ref, st_scr, b_scr, kk_scr, oi_scr, o_scr = rest
        sout2_ref[0, 0] = sprev_ref[0]
        sout_ref = sout2_ref.at[1]
    else:
        o_ref, sout_ref, st_scr, b_scr, kk_scr, oi_scr, o_scr = rest
    _hgrn_body(q_ref, f_ref, i_ref, g_ref, lb_ref, gn_ref, s0_ref, o_ref, sout_ref,
               st_scr, b_scr, kk_scr, oi_scr, o_scr, chunk=chunk, sub=sub, single=single, safe_decay=safe_decay)


def _hgrn_body(q_ref, f_ref, i_ref, g_ref, lb_ref, gn_ref, s0_ref, o_ref, sout_ref,
               st_scr, b_scr, kk_scr, oi_scr, o_scr, *, chunk, sub, single, safe_decay):
    t = pl.program_id(1)
    if not single:
        @pl.when(t == 0)
        def _():
            for h in range(HG_HEADS):
                st_scr[h] = s0_ref[0, h].T

    lb = lb_ref[...]
    q = q_ref[0]
    fl = f_ref[0]
    v = i_ref[0]
    e = jnp.exp(-jnp.abs(fl))
    d = 1.0 + e
    r = 1.0 / d
    c = jnp.log1p(-lb) + (jnp.minimum(fl, 0.0) - jnp.log(d))
    a = jnp.log(lb)
    log_f = jnp.maximum(a, c) + jnp.log(1.0 + jnp.exp(-jnp.abs(a - c)))
    kk = (1.0 - lb) * jnp.where(fl >= 0.0, e * r, r)
    ri = lax.broadcasted_iota(jnp.int32, (chunk, chunk), 0)
    ci = lax.broadcasted_iota(jnp.int32, (chunk, chunk), 1)
    tril = jnp.where(ri >= ci, 1.0, 0.0).astype(BF16)
    p1 = log_f.astype(BF16)
    r1 = log_f - p1.astype(F32)
    p2 = r1.astype(BF16)
    p3 = (r1 - p2.astype(F32)).astype(BF16)
    b3 = _dot(tril, jnp.concatenate([p1, p2, p3], axis=1))
    b = b3[:, 0:MIX_HALF] + b3[:, MIX_HALF:2 * MIX_HALF] + b3[:, 2 * MIX_HALF:3 * MIX_HALF]
    b_scr[...] = b
    kk_scr[...] = kk
    b_last = b[chunk - 1:chunk, :]
    qe = (q * jnp.exp(b)).astype(BF16)
    kd = kk * jnp.exp(b_last - b)
    dec = jnp.exp(b_last)
    npad = HG_D - chunk
    prow = lax.broadcasted_iota(jnp.int32, (HG_D, 1), 0)
    for h in range(HG_HEADS):
        sl = slice(h * HG_D, (h + 1) * HG_D)
        if single:
            s = s0_ref[0, h]
            oi_scr[:, sl] = _dot(qe[:, sl], s.astype(BF16))
            zpad = jnp.zeros((npad, HG_D), F32)
            m = jnp.concatenate([kd[:, sl], zpad], axis=0)
            m = jnp.where(prow == chunk, dec[:, sl], m)
            mt = m.T
            vpad = jnp.concatenate([v[:, sl], zpad], axis=0)
            sout_ref[0, h] = mt[:, chunk:chunk + 1] * s + _dot(mt.astype(BF16), vpad.astype(BF16))
        else:
            st = st_scr[h]
            oi_scr[:, sl] = _dot_nt(qe[:, sl], st.astype(BF16))
            st_scr[h] = dec[:, sl] * st + _dot(v[:, sl].T.astype(BF16), kd[:, sl].astype(BF16))

    n_sub = chunk // sub
    worst = jnp.zeros((1, MIX_HALF), F32)
    if single:
        zpad = jnp.zeros((npad, MIX_HALF), F32)
        kk_full = jnp.concatenate([kk, zpad], axis=0)
        b_full = jnp.concatenate([b, zpad], axis=0)
        v_full = jnp.concatenate([v, zpad], axis=0)
    else:
        kk_full, b_full, v_full = kk, b, v
    anchors = [b[i * sub:i * sub + 1, :] for i in range(n_sub)]
    for i in range(n_sub):
        worst = jnp.maximum(worst, anchors[i] - b[(i + 1) * sub - 1:(i + 1) * sub, :])
    anchor_rows = jnp.concatenate([jnp.broadcast_to(an, (sub, MIX_HALF)) for an in anchors], axis=0)
    qt = (q * jnp.exp(b - anchor_rows)).astype(BF16)
    kts = []
    for i, an in enumerate(anchors):
        n = kk_full.shape[0] if single else (i + 1) * sub
        kt = (kk_full[0:n] * jnp.exp(jnp.minimum(an - b_full[0:n], safe_decay))).astype(BF16)
        if n < kk_full.shape[0]:
            kt = jnp.concatenate([kt, jnp.zeros((kk_full.shape[0] - n, MIX_HALF), BF16)], axis=0)
        kts.append(kt)
    row_blk = lax.broadcasted_iota(jnp.int32, (chunk, HG_D), 0) // sub
    causal = (lax.broadcasted_iota(jnp.int32, (chunk, HG_D), 1)
              <= lax.broadcasted_iota(jnp.int32, (chunk, HG_D), 0))
    zero_q = jnp.zeros((chunk, HG_D), BF16)
    atts = []
    for h in range(HG_HEADS):
        sl = slice(h * HG_D, (h + 1) * HG_D)
        q_cat = jnp.concatenate([jnp.where(row_blk == i, qt[:, sl], zero_q) for i in range(n_sub)], axis=1)
        k_cat = jnp.concatenate([kt[:, sl] for kt in kts], axis=1)
        atts.append(jnp.where(causal, _dot_nt(q_cat, k_cat), 0.0).astype(BF16))
    bd = (lax.broadcasted_iota(jnp.int32, (HG_HEADS * HG_D, MIX_HALF), 0) // HG_D
          == lax.broadcasted_iota(jnp.int32, (HG_HEADS * HG_D, MIX_HALF), 1) // HG_D)
    v_bd = jnp.where(bd, jnp.concatenate([v_full] * HG_HEADS, axis=0), 0.0).astype(BF16)
    o_scr[...] = oi_scr[...] + _dot(jnp.concatenate(atts, axis=1), v_bd)

    @pl.when(jnp.max(worst) > safe_decay)
    def _():
        for h in range(HG_HEADS):
            _hgrn_direct(h, q_ref, i_ref, b_scr, kk_scr, oi_scr, o_scr, chunk=chunk)

    g = g_ref[0]
    gn = gn_ref[...]
    for h in range(HG_HEADS):
        sl = slice(h * HG_D, (h + 1) * HG_D)
        gh = g[:, sl]
        o_ref[0, :, sl] = _rms(o_scr[:, sl], gn[:, sl]) * (gh * _sigmoid(gh))

    if not single:
        @pl.when(t == pl.num_programs(1) - 1)
        def _():
            for h in range(HG_HEADS):
                sout_ref[0, h] = st_scr[h].T


def _hgrn(qfig, lb_l, gn_l, s0, s0_layer=0, s_prev=None, safe_decay=HG_SAFE_DECAY):
    bx, tx, _ = qfig.shape
    chunk = min(tx, HG_CHUNK)
    sub = min(chunk, HG_SUB)
    single = tx == chunk
    assert not single or chunk < HG_D

    def col(k):
        return pl.BlockSpec((1, chunk, MIX_HALF), lambda b, t: (b, t, k))

    vec = pl.BlockSpec((1, MIX_HALF), lambda b, t: (0, 0))
    state = pl.BlockSpec((1, HG_HEADS, HG_D, HG_D), lambda b, t: (b, 0, 0, 0))
    state_in = pl.BlockSpec((None, 1, HG_HEADS, HG_D, HG_D), lambda b, t: (s0_layer, b, 0, 0, 0))
    stacked = s_prev is not None
    if stacked:
        state_out = pl.BlockSpec((2, 1, HG_HEADS, HG_D, HG_D), lambda b, t: (0, b, 0, 0, 0))
        state_shape = jax.ShapeDtypeStruct((2, bx, HG_HEADS, HG_D, HG_D), F32)
    else:
        state_out = state
        state_shape = jax.ShapeDtypeStruct((bx, HG_HEADS, HG_D, HG_D), F32)
    return pl.pallas_call(
        functools.partial(_hgrn_kernel, chunk=chunk, sub=sub, single=single, stacked=stacked, safe_decay=safe_decay),
        grid=(bx, tx // chunk),
        in_specs=[col(0), col(1), col(2), col(3), vec, vec, state_in] + ([state] if stacked else []),
        out_specs=[pl.BlockSpec((1, chunk, MIX_HALF), lambda b, t: (b, t, 0)), state_out],
        out_shape=[jax.ShapeDtypeStruct((bx, tx, MIX_HALF), F32), state_shape],
        scratch_shapes=[pltpu.VMEM((HG_HEADS, HG_D, HG_D), F32)] + [pltpu.VMEM((chunk, MIX_HALF), F32)] * 4,
        compiler_params=_cparams("arbitrary", "arbitrary"),
        name="hgrn",
    )(qfig, qfig, qfig, qfig, lb_l, gn_l, s0, *([s_prev] if stacked else []))


def _sb_prompt_kernel(bias_ref, q_ref, kt_ref, vt_ref, o_ref, acc_scr, carry_scr, *, blk, per_trip):
    g = pl.program_id(1)
    qi = pl.program_id(2)
    lane = lax.broadcasted_iota(jnp.int32, (1, LANES), 1)
    ri = lax.broadcasted_iota(jnp.int32, (blk, blk), 0)
    ci = lax.broadcasted_iota(jnp.int32, (blk, blk), 1)
    upper = jnp.where(ri > ci, 1.0, 0.0).astype(BF16)
    ones = jnp.ones((blk, LANES), BF16)
    tri = ci < ri
    causal = jnp.concatenate([tri, tri], axis=0)
    second = lax.broadcasted_iota(jnp.int32, (2 * blk, 1), 0) >= blk
    qs, bias = [], []
    for p in range(SB_PAIRS):
        q = q_ref[0, :, p * LANES:(p + 1) * LANES]
        zero = jnp.zeros_like(q)
        qs.append(jnp.concatenate([jnp.where(lane < SB_HD, q, zero), jnp.where(lane >= SB_HD, q, zero)], axis=0))
        h0 = 2 * (g * SB_PAIRS + p)
        bias.append(jnp.where(second, bias_ref[h0 + 1], bias_ref[h0]))
    acc_scr[...] = jnp.zeros_like(acc_scr)
    carry_scr[...] = jnp.zeros_like(carry_scr)

    def step(kbs, masked):
        for p in range(SB_PAIRS):
            carry = carry_scr[p]
            pv = None
            for kb in kbs:
                c0 = pl.multiple_of(kb * blk, blk)
                kblk = kt_ref[0, p * LANES:(p + 1) * LANES, pl.ds(c0, blk)]
                vblk = vt_ref[0, p * LANES:(p + 1) * LANES, pl.ds(c0, blk)]
                ls, lk = _log_sigmoid_pair(_dot(qs[p], kblk) + bias[p])
                if masked:
                    lk = jnp.where(causal, lk, 0.0)
                lkb = lk.astype(BF16)
                w = jnp.exp(ls + _dot(lkb, upper) + jnp.concatenate([carry] * (blk // LANES), axis=1))
                if masked:
                    w = jnp.where(causal, w, 0.0)
                pv_k = _dot_nt(w.astype(BF16), vblk)
                pv = pv_k if pv is None else pv + pv_k
                carry = carry + _dot(lkb, ones)
            acc_scr[p] += pv
            carry_scr[p] = carry

    step([qi], True)

    def body(j, c):
        kb = qi - 1 - per_trip * j
        step([kb - i for i in range(per_trip)], False)
        return c

    lax.fori_loop(0, qi // per_trip, body, 0)
    left = qi % per_trip

    def tail(j, c):
        step([left - 1 - j], False)
        return c

    lax.fori_loop(0, left, tail, 0)
    for p in range(SB_PAIRS):
        o_ref[0, :, p * LANES:(p + 1) * LANES] = jnp.where(lane < SB_HD, acc_scr[p, 0:blk], acc_scr[p, blk:2 * blk])


def _sb_prompt(bias_l, q_bf16, kt_bf16, vt_bf16, per_trip):
    bx, tx, _ = q_bf16.shape
    blk = SB_BLOCK
    wd = SB_PAIRS * LANES
    rows = pl.BlockSpec((1, blk, wd), lambda b, g, qi: (b, qi, g))
    full = pl.BlockSpec((1, wd, tx), lambda b, g, qi: (b, g, 0))
    return pl.pallas_call(
        functools.partial(_sb_prompt_kernel, blk=blk, per_trip=per_trip),
        grid=(bx, MIX_HALF // wd, tx // blk),
        in_specs=[pl.BlockSpec(memory_space=pltpu.SMEM), rows, full, full],
        out_specs=rows,
        out_shape=jax.ShapeDtypeStruct((bx, tx, MIX_HALF), F32),
        scratch_shapes=[pltpu.VMEM((SB_PAIRS, 2 * blk, LANES), F32), pltpu.VMEM((SB_PAIRS, 2 * blk, LANES), F32)],
        compiler_params=_cparams("arbitrary", "arbitrary", "arbitrary"),
        name="sb_prompt",
    )(bias_l, q_bf16, kt_bf16, vt_bf16)


def _sb_sample_kernel(pt_ref, q_ref, kn_ref, vn_ref, bias_ref, *rest, tq, page, n_pages):
    kc_refs, vc_refs, o_ref = rest[:n_pages], rest[n_pages:2 * n_pages], rest[2 * n_pages]
    nrow = SB_HEADS * tq
    rowh = lax.broadcasted_iota(jnp.int32, (nrow, MIX_HALF), 0) // tq
    colh = lax.broadcasted_iota(jnp.int32, (nrow, MIX_HALF), 1) // SB_HD
    own = rowh == colh
    bias = bias_ref[...]
    ri = lax.broadcasted_iota(jnp.int32, (page, page), 0)
    ci = lax.broadcasted_iota(jnp.int32, (page, page), 1)
    upper = jnp.where(ri > ci, 1.0, 0.0).astype(BF16)
    qbd = (jnp.where(own, jnp.concatenate([q_ref[0]] * SB_HEADS, axis=0), 0.0) * SB_SCALE).astype(BF16)
    pad = jnp.zeros((page - tq, MIX_HALF), F32)
    kn = jnp.concatenate([kn_ref[0], pad], axis=0).astype(BF16)
    vn = jnp.concatenate([vn_ref[0], pad], axis=0).astype(BF16)
    qpos = lax.broadcasted_iota(jnp.int32, (nrow, page), 0) % tq
    kpos = lax.broadcasted_iota(jnp.int32, (nrow, page), 1)
    new_mask = kpos < qpos

    ls0, lk0 = _log_sigmoid_pair(_dot_nt(qbd, kn) + bias)
    ls_all, lk_all = [ls0], [jnp.where(new_mask, lk0, 0.0)]
    for j in range(n_pages):
        ls, lk = _log_sigmoid_pair(_dot(qbd, kc_refs[j][0, 0].astype(BF16)) + bias)
        ls_all.append(ls)
        lk_all.append(lk)
    lk_cat = jnp.concatenate(lk_all, axis=0)
    suffix = _split_dot(lk_cat, upper, 2)
    total = jnp.sum(lk_cat, axis=-1, keepdims=True)
    carry = jnp.zeros((nrow, 1), F32)
    acc = jnp.zeros((nrow, MIX_HALF), F32)
    for j in range(n_pages + 1):
        r = slice(j * nrow, (j + 1) * nrow)
        w = jnp.exp(ls_all[j] + suffix[r] + carry)
        if j == 0:
            acc = acc + _dot(jnp.where(new_mask, w, 0.0).astype(BF16), vn)
        else:
            acc = acc + _dot_nt(w.astype(BF16), vc_refs[j - 1][0, 0].astype(BF16))
        carry = carry + total[r]
    masked = jnp.where(own, acc, 0.0)
    out = masked[0:tq, :]
    for h in range(1, SB_HEADS):
        out = out + masked[h * tq:(h + 1) * tq, :]
    o_ref[0] = out


def _sb_sample(layer, page_table, bias_rows, q, k_new, v_new, cache_k, cache_v):
    bx, tq, _ = q.shape
    n_pages = page_table.shape[1]
    page = cache_k.shape[3]
    nrow = SB_HEADS * tq
    seq = pl.BlockSpec((1, tq, MIX_HALF), lambda b, pt: (b, 0, 0))

    def cache(j):
        return pl.BlockSpec((1, 1, MIX_HALF, page),
                            lambda b, pt: (layer, pt[b * n_pages + (n_pages - 1 - j)], 0, 0))

    pages = [cache(j) for j in range(n_pages)]
    return pl.pallas_call(
        functools.partial(_sb_sample_kernel, tq=tq, page=page, n_pages=n_pages),
        grid_spec=pltpu.PrefetchScalarGridSpec(
            num_scalar_prefetch=1,
            grid=(bx,),
            in_specs=[seq, seq, seq, pl.BlockSpec((nrow, 1), lambda b, pt: (0, 0))] + pages + pages,
            out_specs=seq,
        ),
        out_shape=jax.ShapeDtypeStruct((bx, tq, MIX_HALF), F32),
        compiler_params=_cparams("arbitrary"),
        name="sb_sample",
    )(page_table.reshape(-1), q, k_new, v_new, bias_rows, *([cache_k] * n_pages), *([cache_v] * n_pages))


def _odd_kernel(u_ref, a_ref, gt_ref, hu_ref, hg_ref, pw_ref, ps_ref, cw_ref, cb_ref, lg_ref, lbias_ref,
                po_ref, co_ref, nu_ref, ng_ref, eu_scr, eg_scr, sh_scr, *, tm, q_start):
    t = pl.program_id(1)
    hu0 = 2 * SUBLANES
    hg0 = 4 * SUBLANES

    @pl.when(t == 0)
    def _():
        eu_scr[0:hu0 - POOL_HIST, :] = jnp.zeros((hu0 - POOL_HIST, MIX_HALF), F32)
        eg_scr[0:hg0 - CONV_HIST, :] = jnp.zeros((hg0 - CONV_HIST, MIX_HALF), F32)
        eu_scr[hu0 - POOL_HIST:hu0, :] = hu_ref[0]
        eg_scr[hg0 - CONV_HIST:hg0, :] = hg_ref[0]

    @pl.when(t > 0)
    def _():
        eu_scr[0:hu0, :] = eu_scr[tm:tm + hu0, :]
        eg_scr[0:hg0, :] = eg_scr[tm:tm + hg0, :]

    u = u_ref[0]
    gt = gt_ref[0]
    glu = a_ref[0] * _sigmoid(gt)
    eu_scr[hu0:hu0 + tm, :] = u
    eg_scr[hg0:hg0 + tm, :] = glu

    pos = q_start + t * tm + lax.broadcasted_iota(jnp.int32, (tm, 1), 0)
    for gi, w in enumerate(POOL_WINDOWS):
        c0 = gi * POOL_GC
        wsum = u[:, c0:c0 + POOL_GC]
        for d in range(1, w):
            wsum = wsum + eu_scr[hu0 - d:hu0 - d + tm, c0:c0 + POOL_GC]
        cnt = jnp.minimum(w, pos + 1).astype(F32)
        pooled = wsum / cnt - u[:, c0:c0 + POOL_GC]
        po = _dot(pooled.astype(BF16), pw_ref[gi].astype(BF16))
        po_ref[0, :, c0:c0 + POOL_GC] = po * ps_ref[:, c0:c0 + POOL_GC]

    cv = jnp.broadcast_to(cb_ref[...], (tm, MIX_HALF))
    first = hg0 - CONV_HIST
    for r in range(SUBLANES):
        taps = [j for j in range(CONV_WIDTH) if (first + j) % SUBLANES == r]
        if not taps:
            continue
        span = (first + taps[-1]) // SUBLANES * SUBLANES
        if r == 0:
            src = eg_scr
        else:
            sh_scr[0:span + tm, :] = eg_scr[r:r + span + tm, :]
            src = sh_scr
        for j in taps:
            a0 = first + j - r
            cv = cv + cw_ref[j:j + 1, :] * src[a0:a0 + tm, :]
    mu = jnp.mean(cv, axis=-1, keepdims=True)
    dv = cv - mu
    var = jnp.mean(dv * dv, axis=-1, keepdims=True)
    y = dv * lax.rsqrt(var + EPS) * lg_ref[...] + lbias_ref[...]
    co_ref[0] = y * _sigmoid(y)

    @pl.when(t == pl.num_programs(1) - 1)
    def _():
        nu_ref[0] = eu_scr[hu0 + tm - POOL_HIST:hu0 + tm, :]
        ng_ref[0] = eg_scr[hg0 + tm - CONV_HIST:hg0 + tm, :]


def _odd(u, a, gt, hist_u, hist_g, io, pool_w, pool_scale, conv_w, conv_b, ln_g, ln_b, q_start):
    bx, tx, _ = u.shape
    tm = min(tx, 256)
    tile = pl.BlockSpec((1, tm, MIX_HALF), lambda b, t: (b, t, 0))
    vec = pl.BlockSpec((1, MIX_HALF), lambda b, t: (0, 0))
    hu_spec = pl.BlockSpec((None, 1, POOL_HIST, MIX_HALF), lambda b, t: (io if hist_u.shape[0] > 1 else 0, b, 0, 0))
    hg_spec = pl.BlockSpec((None, 1, CONV_HIST, MIX_HALF), lambda b, t: (io if hist_g.shape[0] > 1 else 0, b, 0, 0))
    return pl.pallas_call(
        functools.partial(_odd_kernel, tm=tm, q_start=q_start),
        grid=(bx, tx // tm),
        in_specs=[tile, tile, tile, hu_spec, hg_spec,
                  pl.BlockSpec((None, len(POOL_WINDOWS), POOL_GC, POOL_GC), lambda b, t: (io, 0, 0, 0)),
                  vec,
                  pl.BlockSpec((None, CONV_WIDTH, MIX_HALF), lambda b, t: (io, 0, 0)),
                  vec, vec, vec],
        out_specs=[tile, tile,
                   pl.BlockSpec((1, POOL_HIST, MIX_HALF), lambda b, t: (b, 0, 0)),
                   pl.BlockSpec((1, CONV_HIST, MIX_HALF), lambda b, t: (b, 0, 0))],
        out_shape=[jax.ShapeDtypeStruct((bx, tx, MIX_HALF), F32), jax.ShapeDtypeStruct((bx, tx, MIX_HALF), F32),
                   jax.ShapeDtypeStruct((bx, POOL_HIST, MIX_HALF), F32),
                   jax.ShapeDtypeStruct((bx, CONV_HIST, MIX_HALF), F32)],
        scratch_shapes=[pltpu.VMEM((2 * SUBLANES + tm, MIX_HALF), F32),
                        pltpu.VMEM((4 * SUBLANES + tm, MIX_HALF), F32),
                        pltpu.VMEM((4 * SUBLANES + tm, MIX_HALF), F32)],
        compiler_params=_cparams("arbitrary", "arbitrary"),
        name="odd_mixer",
    )(u, a, gt, hist_u, hist_g, pool_w, pool_scale[io:io + 1], conv_w, conv_b[io:io + 1], ln_g[io:io + 1],
      ln_b[io:io + 1])


def _post_kernel(a_ref, b_ref, x_ref, g1_ref, sh2_ref, sc2_ref, g2_ref, ng_ref, woa_ref, wob_ref,
                 w1_ref, w2_ref, o_ref, *, tm):
    def rows_of(ref, rows):
        return ref[0] if ref.shape[1] == 1 else ref[0, rows, :]

    for r0 in range(0, tm, POST_GROUP):
        rows = slice(r0, r0 + POST_GROUP)
        m = _dot(a_ref[0, rows, :].astype(BF16), woa_ref[...]) + _dot(b_ref[0, rows, :].astype(BF16), wob_ref[...])
        x1 = x_ref[0, rows, :] + rows_of(g1_ref, rows) * _rms(m, ng_ref[1:2, :])
        h2 = (_rms(x1, ng_ref[2:3, :]) * (1.0 + rows_of(sc2_ref, rows)) + rows_of(sh2_ref, rows)).astype(BF16)
        acc = None
        for c0 in range(0, D_FF, FF_TILE):
            hc = jnp.maximum(_dot(h2, w1_ref[:, c0:c0 + FF_TILE]), 0.0)
            d = _dot((hc * hc).astype(BF16), w2_ref[c0:c0 + FF_TILE, :])
            acc = d if acc is None else acc + d
        o_ref[0, rows, :] = x1 + rows_of(g2_ref, rows) * _rms(acc, ng_ref[3:4, :])


def _post(a, b, x, mod, norm_g, layer, wo_bf16, wo_layer, w1_bf16, w2_bf16):
    bx, tx, _ = x.shape
    tm = min(tx, ROW_TILE)
    half = pl.BlockSpec((1, tm, MIX_HALF), lambda b_, t: (b_, t, 0))
    full = pl.BlockSpec((1, tm, D_MODEL), lambda b_, t: (b_, t, 0))
    once = pl.Buffered(1)
    return pl.pallas_call(
        functools.partial(_post_kernel, tm=tm),
        grid=(bx, tx // tm),
        in_specs=[half, half, full,
                  _mod_spec(mod, tm, 2), _mod_spec(mod, tm, 3), _mod_spec(mod, tm, 4), _mod_spec(mod, tm, 5),
                  pl.BlockSpec((None, 4, D_MODEL), lambda b_, t: (layer, 0, 0)),
                  pl.BlockSpec((None, MIX_HALF, D_MODEL), lambda b_, t: (wo_layer, 0, 0), pipeline_mode=once),
                  pl.BlockSpec((None, MIX_HALF, D_MODEL), lambda b_, t: (wo_layer, 1, 0), pipeline_mode=once),
                  pl.BlockSpec((None, D_MODEL, D_FF), lambda b_, t: (layer, 0, 0), pipeline_mode=once),
                  pl.BlockSpec((None, D_FF, D_MODEL), lambda b_, t: (layer, 0, 0), pipeline_mode=once)],
        out_specs=full,
        out_shape=jax.ShapeDtypeStruct((bx, tx, D_MODEL), F32),
        compiler_params=_cparams("arbitrary", "arbitrary"),
        name="post",
    )(a, b, x, mod, mod, mod, mod, norm_g, wo_bf16, wo_bf16, w1_bf16, w2_bf16)


def kernel(x_prompt, x_sample, c_prompt, c_sample, cache_sb_k, cache_sb_v, state_hgrn, state_pool, state_conv,
           page_table, norm_g, ada_w, ada_b, even_w_in, even_w_out, sb_bias, hgrn_lb_logits, hgrn_norm_g,
           odd_w_in, odd_w_out, pool_w, pool_scale, conv_w, conv_b, conv_ln_g, conv_ln_b, mlp_w1, mlp_w2):
    n_even = even_w_in.shape[0]
    assert n_even == 2 and DEPTH == 4
    bp, tp, _ = x_prompt.shape
    bs, ts, _ = x_sample.shape
    n_pool, page = cache_sb_k.shape[1], cache_sb_k.shape[2]
    past_len = page_table.shape[1] * page
    half = MIX_HALF

    lb_cum = jnp.cumsum(jax.nn.softmax(hgrn_lb_logits.astype(F32), axis=0), axis=0)
    lower_bounds = jnp.maximum(lb_cum - lb_cum[:1], 0.0)

    mod_all = _ada(jnp.concatenate([c_prompt, c_sample], axis=0), ada_w, ada_b)
    even_w_in_b = even_w_in.astype(BF16)
    even_w_out_b = even_w_out.astype(BF16)
    odd_w_in_b = odd_w_in.astype(BF16)
    odd_w_out_b = odd_w_out.astype(BF16)
    w1_b = mlp_w1.astype(BF16)
    w2_b = mlp_w2.astype(BF16)
    cache_k = cache_sb_k.transpose(0, 1, 3, 4, 2).reshape(n_even, n_pool, half, page)
    cache_v = cache_sb_v.transpose(0, 1, 3, 4, 2).reshape(n_even, n_pool, half, page)
    gn = hgrn_norm_g.reshape(n_even, half)

    def run(x, mods, is_sample):
        bx, tx, _ = x.shape
        s_list, k_list, v_list, p_list, c_list = [], [], [], [], []
        for l in range(DEPTH):
            mod = mods[l]
            if l % 2 == 0:
                ie = l // 2
                if is_sample:
                    qfig, q_b, k_b, v_b = _in_proj(
                        x, mod, norm_g, l, even_w_in_b, ie,
                        [(0, 4 * half, "rows"), (4 * half, half, "rows"), (5 * half, half, "rows"),
                         (6 * half, half, "rows")])
                    o_a, s_new = _hgrn(qfig.reshape(bs, ts, 4 * half), lower_bounds[ie:ie + 1], gn[ie:ie + 1],
                                       state_hgrn, ie, s_prev=s_list[0] if ie == 1 else None)
                    bias_rows = jnp.repeat(sb_bias[ie].astype(F32), ts).reshape(SB_HEADS * ts, 1)
                    k_b = k_b.reshape(bs, ts, half)
                    v_b = v_b.reshape(bs, ts, half)
                    o_b = _sb_sample(ie, page_table, bias_rows, q_b.reshape(bs, ts, half), k_b, v_b, cache_k, cache_v)
                    k_list.append(k_b.reshape(bs, ts, SB_HEADS, SB_HD))
                    v_list.append(v_b.reshape(bs, ts, SB_HEADS, SB_HD))
                else:
                    kv_kind = "cols" if ie == 0 else "cols_after"
                    qfig, q_b, k_t, k_tb, v_t, v_tb = _in_proj(
                        x, mod, norm_g, l, even_w_in_b, ie,
                        [(0, 4 * half, "rows"), (4 * half, half, "rows_q"), (5 * half, half, kv_kind),
                         (5 * half, half, "cols_bf16"), (6 * half, half, kv_kind), (6 * half, half, "cols_bf16")],
                        prev=() if ie == 0 else (k_list[0], v_list[0]))
                    o_a, s_new = _hgrn(qfig, lower_bounds[ie:ie + 1], gn[ie:ie + 1],
                                       jnp.zeros((1, bx, HG_HEADS, HG_D, HG_D), F32))
                    o_b = _sb_prompt(sb_bias[ie].astype(F32), q_b, k_tb, v_tb, per_trip=2 if ie == 0 else 4)
                    k_list.append(k_t)
                    v_list.append(v_t)
                mix_a, mix_b, w_out, wo_layer = o_a.reshape(bx, tx, half), o_b.reshape(bx, tx, half), even_w_out_b, ie
                s_list.append(s_new)
            else:
                io = l // 2
                u, a, gt = _in_proj(x, mod, norm_g, l, odd_w_in_b, io,
                                    [(0, half, "rows"), (half, half, "rows"), (2 * half, half, "rows")])
                if is_sample:
                    hb, ht = bs, ts
                    hist_u, hist_g, q_start = state_pool, state_conv, past_len
                else:
                    hb, ht = bx, tx
                    hist_u = jnp.zeros((1, bx, POOL_HIST, half), F32)
                    hist_g = jnp.zeros((1, bx, CONV_HIST, half), F32)
                    q_start = 0
                po, co, new_u, new_g = _odd(u.reshape(hb, ht, half), a.reshape(hb, ht, half),
                                            gt.reshape(hb, ht, half), hist_u, hist_g, io, pool_w, pool_scale,
                                            conv_w, conv_b, conv_ln_g, conv_ln_b, q_start)
                mix_a, mix_b, w_out, wo_layer = po.reshape(bx, tx, half), co.reshape(bx, tx, half), odd_w_out_b, io
                p_list.append(new_u)
                c_list.append(new_g)
            x = _post(mix_a, mix_b, x, mod, norm_g, l, w_out, wo_layer, w1_b, w2_b)
        if is_sample:
            s_all, k_all, v_all = s_list[1], jnp.stack(k_list), jnp.stack(v_list)
        else:
            s_all = jnp.stack(s_list)
            k_all = k_list[1].reshape(2, bx, SB_HEADS, SB_HD, tx).transpose(0, 1, 4, 2, 3)
            v_all = v_list[1].reshape(2, bx, SB_HEADS, SB_HD, tx).transpose(0, 1, 4, 2, 3)
        return (x, s_all, k_all, v_all, jnp.stack(p_list), jnp.stack(c_list))

    mods_p = [mod_all[l, :bp].reshape(bp, 1, 6 * D_MODEL) for l in range(DEPTH)]
    mods_s = [jnp.repeat(mod_all[l, bp:], ts, axis=0).reshape(1, bs * ts, 6 * D_MODEL) for l in range(DEPTH)]
    y_p, s_p, k_p, v_p, pool_p, conv_p = run(x_prompt, mods_p, False)
    y_s, s_s, k_s, v_s, pool_s, conv_s = run(x_sample.reshape(1, bs * ts, D_MODEL), mods_s, True)
    return (y_p, y_s.reshape(bs, ts, D_MODEL), k_p, v_p, k_s, v_s, s_p, s_s, pool_p, pool_s, conv_p, conv_s)
```

```python
import functools

import jax
import jax.numpy as jnp
from jax import lax
from jax.experimental import pallas as pl
from jax.experimental.pallas import tpu as pltpu

F32 = jnp.float32
BF16 = jnp.bfloat16

D_MODEL = 1024
DEPTH = 4
MIX_HALF = D_MODEL // 2
HG_HEADS = 4
HG_D = MIX_HALF // HG_HEADS
SB_HEADS = 8
SB_HD = MIX_HALF // SB_HEADS
SB_SCALE = SB_HD ** -0.5
POOL_WINDOWS = (2, 4, 8, 16)
POOL_GC = MIX_HALF // len(POOL_WINDOWS)
POOL_HIST = max(POOL_WINDOWS) - 1
CONV_WIDTH = 31
CONV_HIST = CONV_WIDTH - 1
D_FF = 4 * D_MODEL
EPS = 1e-6

LANES = 128
SUBLANES = 8
VMEM_LIMIT = 52 * 1024 * 1024

ROW_TILE = 512
POST_GROUP = 256
FF_TILE = 1024
HG_CHUNK = 128
HG_SUB = 16
HG_SAFE_DECAY = 72.0
HG_DIRECT_SUB = 16
HG_SEQS = 2
SB_BLOCK = 256
SB_PER_TRIP = 2
SB_PAIRS = 2


def _cparams(*sem):
    return pltpu.CompilerParams(dimension_semantics=sem, vmem_limit_bytes=VMEM_LIMIT)


def _dot(a, b):
    return jnp.dot(a, b, preferred_element_type=F32)


def _dot_nt(a, b):
    return lax.dot_general(a, b, (((1,), (1,)), ((), ())), preferred_element_type=F32)


def _split_dot(x, m_bf16, terms):
    acc = None
    r = x
    for i in range(terms):
        p = r.astype(BF16)
        d = _dot(p, m_bf16)
        acc = d if acc is None else acc + d
        if i + 1 < terms:
            r = r - p.astype(F32)
    return acc


def _split_dot_left(m_bf16, x, terms):
    acc = None
    r = x
    for i in range(terms):
        p = r.astype(BF16)
        d = _dot(m_bf16, p)
        acc = d if acc is None else acc + d
        if i + 1 < terms:
            r = r - p.astype(F32)
    return acc


def _sigmoid(x):
    return 1.0 / (1.0 + jnp.exp(-x))


def _log_sigmoid(x):
    return jnp.minimum(x, 0.0) - jnp.log1p(jnp.exp(-jnp.abs(x)))


def _log_sigmoid_pair(z):
    ls = jnp.minimum(z, 0.0) - jnp.log(1.0 + jnp.exp(-jnp.abs(z)))
    return ls, ls - z


def _rms(x, g):
    return x * lax.rsqrt(jnp.mean(x * x, axis=-1, keepdims=True) + EPS) * g


def _ada_kernel(c_ref, w_ref, b_ref, o_ref):
    c = c_ref[...]
    s = (c * _sigmoid(c)).astype(BF16)
    o_ref[0] = _dot(s, w_ref[0].astype(BF16)) + b_ref[0]


def _ada(c_all, ada_w, ada_b):
    nb = c_all.shape[0]
    tn = 1536
    return pl.pallas_call(
        _ada_kernel,
        grid=(DEPTH, 6 * D_MODEL // tn),
        in_specs=[
            pl.BlockSpec((nb, D_MODEL), lambda l, j: (0, 0)),
            pl.BlockSpec((1, D_MODEL, tn), lambda l, j: (l, 0, j)),
            pl.BlockSpec((1, 1, tn), lambda l, j: (l, 0, j)),
        ],
        out_specs=pl.BlockSpec((1, nb, tn), lambda l, j: (l, 0, j)),
        out_shape=jax.ShapeDtypeStruct((DEPTH, nb, 6 * D_MODEL), F32),
        compiler_params=_cparams("arbitrary", "arbitrary"),
        name="ada",
    )(c_all, ada_w, ada_b.reshape(DEPTH, 1, 6 * D_MODEL))


def _mod_spec(mod, tm, chunk):
    if mod.shape[1] == 1:
        return pl.BlockSpec((1, 1, D_MODEL), lambda b, t, *_: (b, 0, chunk))
    return pl.BlockSpec((1, tm, D_MODEL), lambda b, t, *_: (b, t, chunk))


def _inproj_kernel(x_ref, sh_ref, sc_ref, g_ref, w_ref, *rest, plan):
    n_prev = sum(kind == "cols_after" for _, _, kind in plan)
    prev_refs, o_refs = list(rest[:n_prev]), rest[n_prev:]
    h = _rms(x_ref[0], g_ref[0:1, :]) * (1.0 + sc_ref[0]) + sh_ref[0]
    y = _dot(h.astype(BF16), w_ref[...])
    transposed = {}
    for o_ref, (c0, n, kind) in zip(o_refs, plan):
        blk = y[:, c0:c0 + n]
        if kind == "rows":
            o_ref[0] = blk
        elif kind == "rows_q":
            o_ref[0] = (blk * SB_SCALE).astype(BF16)
        else:
            if c0 not in transposed:
                transposed[c0] = blk.T
            if kind == "cols_after":
                o_ref[0, 0] = prev_refs.pop(0)[0]
                o_ref[1, 0] = transposed[c0]
            else:
                o_ref[0] = transposed[c0].astype(o_ref.dtype)


def _in_proj(x, mod, norm_g, layer, w_bf16, w_layer, plan, prev=()):
    bx, tx, _ = x.shape
    tm = min(tx, ROW_TILE)
    n = w_bf16.shape[2]
    out_specs, out_shape, prev_specs = [], [], []
    for _, wd, kind in plan:
        if kind in ("rows", "rows_q"):
            out_specs.append(pl.BlockSpec((1, tm, wd), lambda b, t: (b, t, 0)))
            out_shape.append(jax.ShapeDtypeStruct((bx, tx, wd), F32 if kind == "rows" else BF16))
        elif kind == "cols_after":
            prev_specs.append(pl.BlockSpec((1, wd, tm), lambda b, t: (b, 0, t)))
            out_specs.append(pl.BlockSpec((2, 1, wd, tm), lambda b, t: (0, b, 0, t)))
            out_shape.append(jax.ShapeDtypeStruct((2, bx, wd, tx), F32))
        else:
            out_specs.append(pl.BlockSpec((1, wd, tm), lambda b, t: (b, 0, t)))
            out_shape.append(jax.ShapeDtypeStruct((bx, wd, tx), F32 if kind == "cols" else BF16))
    assert len(prev_specs) == len(prev)
    return pl.pallas_call(
        functools.partial(_inproj_kernel, plan=tuple(plan)),
        grid=(bx, tx // tm),
        in_specs=[
            pl.BlockSpec((1, tm, D_MODEL), lambda b, t: (b, t, 0)),
            _mod_spec(mod, tm, 0),
            _mod_spec(mod, tm, 1),
            pl.BlockSpec((None, 4, D_MODEL), lambda b, t: (layer, 0, 0)),
            pl.BlockSpec((None, D_MODEL, n), lambda b, t: (w_layer, 0, 0)),
        ] + prev_specs,
        out_specs=out_specs,
        out_shape=out_shape,
        compiler_params=_cparams("arbitrary", "arbitrary"),
        name="in_proj",
    )(x, mod, mod, norm_g, w_bf16, *prev)


def _hgrn_direct(s, h, q_ref, i_ref, b_scr, kk_scr, oi_scr, o_scr, *, chunk):
    sub = min(chunk, HG_DIRECT_SUB)
    sl = slice(h * HG_D, (h + 1) * HG_D)
    rows = lax.broadcasted_iota(jnp.int32, (chunk, 1), 0)
    srow = lax.broadcasted_iota(jnp.int32, (sub, 1), 0)

    def sub_block(i, carry):
        r0 = pl.multiple_of(i * sub, sub)
        b_i = b_scr[s, pl.ds(r0, sub), sl]
        q_i = q_ref[s, pl.ds(r0, sub), sl]
        kk_i = kk_scr[s, pl.ds(r0, sub), sl]
        v_i = i_ref[s, pl.ds(r0, sub), sl]
        o = oi_scr[s, pl.ds(r0, sub), sl]
        if chunk > sub:
            anchor = b_i[0:1, :]
            qt = q_i * jnp.exp(b_i - anchor)
            kt = jnp.where(rows < r0, kk_scr[s, :, sl] * jnp.exp(jnp.minimum(anchor - b_scr[s, :, sl], 0.0)), 0.0)
            a_off = _dot_nt(qt.astype(BF16), kt.astype(BF16))
            o = o + _dot(a_off.astype(BF16), i_ref[s, :, sl].astype(BF16))
        o_diag = jnp.zeros((sub, HG_D), F32)
        for tt in range(sub):
            e = jnp.exp(jnp.minimum(b_i[tt:tt + 1, :] - b_i, 0.0))
            d = jnp.sum(q_i[tt:tt + 1, :] * e * kk_i, axis=-1, keepdims=True)
            d = jnp.where(srow <= tt, d, 0.0)
            row = jnp.sum(d * v_i, axis=0, keepdims=True)
            o_diag = jnp.where(srow == tt, row, o_diag)
        o_scr[s, pl.ds(r0, sub), sl] = o + o_diag
        return carry

    lax.fori_loop(0, chunk // sub, sub_block, 0)


def _hgrn_kernel(q_ref, f_ref, i_ref, g_ref, lb_ref, gn_ref, s0_ref, *rest,
                 nb, chunk, sub, single, stacked, safe_decay):
    if stacked:
        sprev_ref, o_ref, sout2_ref, st_scr, b_scr, kk_scr, oi_scr, o_scr = rest
        sout2_ref[0] = sprev_ref[...]
        sout_ref = sout2_ref.at[1]
    else:
        o_ref, sout_ref, st_scr, b_scr, kk_scr, oi_scr, o_scr = rest
    t = pl.program_id(1)
    if not single:
        @pl.when(t == 0)
        def _():
            for s in range(nb):
                for h in range(HG_HEADS):
                    st_scr[s * HG_HEADS + h] = s0_ref[s, h].T

    worst = [_hgrn_main(s, q_ref, f_ref, i_ref, lb_ref, s0_ref, sout_ref, st_scr, b_scr, kk_scr, oi_scr, o_scr,
                        chunk=chunk, sub=sub, single=single, safe_decay=safe_decay) for s in range(nb)]

    for s in range(nb):
        @pl.when(jnp.max(worst[s]) > safe_decay)
        def _(s=s):
            for h in range(HG_HEADS):
                _hgrn_direct(s, h, q_ref, i_ref, b_scr, kk_scr, oi_scr, o_scr, chunk=chunk)

    gn = gn_ref[...]
    for s in range(nb):
        g = g_ref[s]
        for h in range(HG_HEADS):
            sl = slice(h * HG_D, (h + 1) * HG_D)
            gh = g[:, sl]
            o_ref[s, :, sl] = _rms(o_scr[s, :, sl], gn[:, sl]) * (gh * _sigmoid(gh))

    if not single:
        @pl.when(t == pl.num_programs(1) - 1)
        def _():
            for s in range(nb):
                for h in range(HG_HEADS):
                    sout_ref[s, h] = st_scr[s * HG_HEADS + h].T


def _hgrn_main(s, q_ref, f_ref, i_ref, lb_ref, s0_ref, sout_ref, st_scr, b_scr, kk_scr, oi_scr, o_scr,
               *, chunk, sub, single, safe_decay):
    lb = lb_ref[...]
    q = q_ref[s]
    fl = f_ref[s]
    v = i_ref[s]
    e = jnp.exp(-jnp.abs(fl))
    d = 1.0 + e
    r = 1.0 / d
    c = jnp.log1p(-lb) + (jnp.minimum(fl, 0.0) - jnp.log(d))
    a = jnp.log(lb)
    log_f = jnp.maximum(a, c) + jnp.log(1.0 + jnp.exp(-jnp.abs(a - c)))
    kk = (1.0 - lb) * jnp.where(fl >= 0.0, e * r, r)
    ri = lax.broadcasted_iota(jnp.int32, (chunk, chunk), 0)
    ci = lax.broadcasted_iota(jnp.int32, (chunk, chunk), 1)
    tril = jnp.where(ri >= ci, 1.0, 0.0).astype(BF16)
    p1 = log_f.astype(BF16)
    r1 = log_f - p1.astype(F32)
    p2 = r1.astype(BF16)
    p3 = (r1 - p2.astype(F32)).astype(BF16)
    b3 = _dot(tril, jnp.concatenate([p1, p2, p3], axis=1))
    b = b3[:, 0:MIX_HALF] + b3[:, MIX_HALF:2 * MIX_HALF] + b3[:, 2 * MIX_HALF:3 * MIX_HALF]
    b_scr[s] = b
    kk_scr[s] = kk
    b_last = b[chunk - 1:chunk, :]
    qe = (q * jnp.exp(b)).astype(BF16)
    kd = kk * jnp.exp(b_last - b)
    dec = jnp.exp(b_last)
    npad = HG_D - chunk
    prow = lax.broadcasted_iota(jnp.int32, (HG_D, 1), 0)
    for h in range(HG_HEADS):
        sl = slice(h * HG_D, (h + 1) * HG_D)
        if single:
            st = s0_ref[s, h]
            oi_scr[s, :, sl] = _dot(qe[:, sl], st.astype(BF16))
            zpad = jnp.zeros((npad, HG_D), F32)
            m = jnp.concatenate([kd[:, sl], zpad], axis=0)
            m = jnp.where(prow == chunk, dec[:, sl], m)
            mt = m.T
            vpad = jnp.concatenate([v[:, sl], zpad], axis=0)
            sout_ref[s, h] = mt[:, chunk:chunk + 1] * st + _dot(mt.astype(BF16), vpad.astype(BF16))
        else:
            st = st_scr[s * HG_HEADS + h]
            oi_scr[s, :, sl] = _dot_nt(qe[:, sl], st.astype(BF16))
            st_scr[s * HG_HEADS + h] = dec[:, sl] * st + _dot(v[:, sl].T.astype(BF16), kd[:, sl].astype(BF16))

    n_sub = chunk // sub
    worst = jnp.zeros((1, MIX_HALF), F32)
    if single:
        zpad = jnp.zeros((npad, MIX_HALF), F32)
        kk_full = jnp.concatenate([kk, zpad], axis=0)
        b_full = jnp.concatenate([b, zpad], axis=0)
        v_full = jnp.concatenate([v, zpad], axis=0)
    else:
        kk_full, b_full, v_full = kk, b, v
    anchors = [b[i * sub:i * sub + 1, :] for i in range(n_sub)]
    for i in range(n_sub):
        worst = jnp.maximum(worst, anchors[i] - b[(i + 1) * sub - 1:(i + 1) * sub, :])
    anchor_rows = jnp.concatenate([jnp.broadcast_to(an, (sub, MIX_HALF)) for an in anchors], axis=0)
    qt = (q * jnp.exp(b - anchor_rows)).astype(BF16)
    kts = []
    for i, an in enumerate(anchors):
        n = kk_full.shape[0] if single else (i + 1) * sub
        kt = (kk_full[0:n] * jnp.exp(jnp.minimum(an - b_full[0:n], safe_decay))).astype(BF16)
        if n < kk_full.shape[0]:
            kt = jnp.concatenate([kt, jnp.zeros((kk_full.shape[0] - n, MIX_HALF), BF16)], axis=0)
        kts.append(kt)
    row_blk = lax.broadcasted_iota(jnp.int32, (chunk, HG_D), 0) // sub
    causal = (lax.broadcasted_iota(jnp.int32, (chunk, HG_D), 1)
              <= lax.broadcasted_iota(jnp.int32, (chunk, HG_D), 0))
    zero_q = jnp.zeros((chunk, HG_D), BF16)
    atts = []
    for h in range(HG_HEADS):
        sl = slice(h * HG_D, (h + 1) * HG_D)
        q_cat = jnp.concatenate([jnp.where(row_blk == i, qt[:, sl], zero_q) for i in range(n_sub)], axis=1)
        k_cat = jnp.concatenate([kt[:, sl] for kt in kts], axis=1)
        atts.append(jnp.where(causal, _dot_nt(q_cat, k_cat), 0.0).astype(BF16))
    bd = (lax.broadcasted_iota(jnp.int32, (HG_HEADS * HG_D, MIX_HALF), 0) // HG_D
          == lax.broadcasted_iota(jnp.int32, (HG_HEADS * HG_D, MIX_HALF), 1) // HG_D)
    v_bd = jnp.where(bd, jnp.concatenate([v_full] * HG_HEADS, axis=0), 0.0).astype(BF16)
    o_scr[s] = oi_scr[s] + _dot(jnp.concatenate(atts, axis=1), v_bd)
    return worst


def _hgrn(qfig, lb_l, gn_l, s0, s0_layer=0, s_prev=None, safe_decay=HG_SAFE_DECAY):
    bx, tx, _ = qfig.shape
    chunk = min(tx, HG_CHUNK)
    sub = min(chunk, HG_SUB)
    single = tx == chunk
    assert not single or chunk < HG_D
    nb = HG_SEQS if bx % HG_SEQS == 0 else 1

    def col(k):
        return pl.BlockSpec((nb, chunk, MIX_HALF), lambda b, t: (b, t, k))

    vec = pl.BlockSpec((1, MIX_HALF), lambda b, t: (0, 0))
    state = pl.BlockSpec((nb, HG_HEADS, HG_D, HG_D), lambda b, t: (b, 0, 0, 0))
    state_in = pl.BlockSpec((None, nb, HG_HEADS, HG_D, HG_D), lambda b, t: (s0_layer, b, 0, 0, 0))
    stacked = s_prev is not None
    if stacked:
        state_out = pl.BlockSpec((2, nb, HG_HEADS, HG_D, HG_D), lambda b, t: (0, b, 0, 0, 0))
        state_shape = jax.ShapeDtypeStruct((2, bx, HG_HEADS, HG_D, HG_D), F32)
    else:
        state_out = state
        state_shape = jax.ShapeDtypeStruct((bx, HG_HEADS, HG_D, HG_D), F32)
    return pl.pallas_call(
        functools.partial(_hgrn_kernel, nb=nb, chunk=chunk, sub=sub, single=single, stacked=stacked,
                          safe_decay=safe_decay),
        grid=(bx // nb, tx // chunk),
        in_specs=[col(0), col(1), col(2), col(3), vec, vec, state_in] + ([state] if stacked else []),
        out_specs=[pl.BlockSpec((nb, chunk, MIX_HALF), lambda b, t: (b, t, 0)), state_out],
        out_shape=[jax.ShapeDtypeStruct((bx, tx, MIX_HALF), F32), state_shape],
        scratch_shapes=([pltpu.VMEM((nb * HG_HEADS, HG_D, HG_D), F32)]
                        + [pltpu.VMEM((nb, chunk, MIX_HALF), F32)] * 4),
        compiler_params=_cparams("arbitrary", "arbitrary"),
        name="hgrn",
    )(qfig, qfig, qfig, qfig, lb_l, gn_l, s0, *([s_prev] if stacked else []))


def _sb_prompt_kernel(bias_ref, q_ref, kt_ref, vt_ref, o_ref, acc_scr, carry_scr, *, blk, per_trip):
    g = pl.program_id(1)
    qi = pl.program_id(2)
    lane = lax.broadcasted_iota(jnp.int32, (1, LANES), 1)
    ri = lax.broadcasted_iota(jnp.int32, (blk, blk), 0)
    ci = lax.broadcasted_iota(jnp.int32, (blk, blk), 1)
    upper = jnp.where(ri > ci, 1.0, 0.0).astype(BF16)
    ones = jnp.ones((blk, LANES), BF16)
    tri = ci < ri
    causal = jnp.concatenate([tri, tri], axis=0)
    second = lax.broadcasted_iota(jnp.int32, (2 * blk, 1), 0) >= blk
    qs, bias = [], []
    for p in range(SB_PAIRS):
        q = q_ref[0, :, p * LANES:(p + 1) * LANES]
        zero = jnp.zeros_like(q)
        qs.append(jnp.concatenate([jnp.where(lane < SB_HD, q, zero), jnp.where(lane >= SB_HD, q, zero)], axis=0))
        h0 = 2 * (g * SB_PAIRS + p)
        bias.append(jnp.where(second, bias_ref[h0 + 1], bias_ref[h0]))
    acc_scr[...] = jnp.zeros_like(acc_scr)
    carry_scr[...] = jnp.zeros_like(carry_scr)

    def step(kbs, masked):
        for p in range(SB_PAIRS):
            carry = carry_scr[p]
            pv = None
            for kb in kbs:
                c0 = pl.multiple_of(kb * blk, blk)
                kblk = kt_ref[0, p * LANES:(p + 1) * LANES, pl.ds(c0, blk)]
                vblk = vt_ref[0, p * LANES:(p + 1) * LANES, pl.ds(c0, blk)]
                ls, lk = _log_sigmoid_pair(_dot(qs[p], kblk) + bias[p])
                if masked:
                    lk = jnp.where(causal, lk, 0.0)
                lkb = lk.astype(BF16)
                w = jnp.exp(ls + _dot(lkb, upper) + jnp.concatenate([carry] * (blk // LANES), axis=1))
                if masked:
                    w = jnp.where(causal, w, 0.0)
                pv_k = _dot_nt(w.astype(BF16), vblk)
                pv = pv_k if pv is None else pv + pv_k
                carry = carry + _dot(lkb, ones)
            acc_scr[p] += pv
            carry_scr[p] = carry

    step([qi], True)

    def body(j, c):
        kb = qi - 1 - per_trip * j
        step([kb - i for i in range(per_trip)], False)
        return c

    lax.fori_loop(0, qi // per_trip, body, 0)
    left = qi % per_trip

    def tail(j, c):
        step([left - 1 - j], False)
        return c

    lax.fori_loop(0, left, tail, 0)
    for p in range(SB_PAIRS):
        o_ref[0, :, p * LANES:(p + 1) * LANES] = jnp.where(lane < SB_HD, acc_scr[p, 0:blk], acc_scr[p, blk:2 * blk])


def _sb_prompt(bias_l, q_bf16, kt_bf16, vt_bf16):
    bx, tx, _ = q_bf16.shape
    blk = SB_BLOCK
    wd = SB_PAIRS * LANES
    rows = pl.BlockSpec((1, blk, wd), lambda b, g, qi: (b, qi, g))
    full = pl.BlockSpec((1, wd, tx), lambda b, g, qi: (b, g, 0))
    return pl.pallas_call(
        functools.partial(_sb_prompt_kernel, blk=blk, per_trip=SB_PER_TRIP),
        grid=(bx, MIX_HALF // wd, tx // blk),
        in_specs=[pl.BlockSpec(memory_space=pltpu.SMEM), rows, full, full],
        out_specs=rows,
        out_shape=jax.ShapeDtypeStruct((bx, tx, MIX_HALF), F32),
        scratch_shapes=[pltpu.VMEM((SB_PAIRS, 2 * blk, LANES), F32), pltpu.VMEM((SB_PAIRS, 2 * blk, LANES), F32)],
        compiler_params=_cparams("arbitrary", "arbitrary", "arbitrary"),
        name="sb_prompt",
    )(bias_l, q_bf16, kt_bf16, vt_bf16)


def _sb_sample_kernel(pt_ref, q_ref, kn_ref, vn_ref, bias_ref, *rest, tq, page, n_pages):
    kc_refs, vc_refs, o_ref = rest[:n_pages], rest[n_pages:2 * n_pages], rest[2 * n_pages]
    nrow = SB_HEADS * tq
    rowh = lax.broadcasted_iota(jnp.int32, (nrow, MIX_HALF), 0) // tq
    colh = lax.broadcasted_iota(jnp.int32, (nrow, MIX_HALF), 1) // SB_HD
    own = rowh == colh
    bias = bias_ref[...]
    ri = lax.broadcasted_iota(jnp.int32, (page, page), 0)
    ci = lax.broadcasted_iota(jnp.int32, (page, page), 1)
    upper = jnp.where(ri > ci, 1.0, 0.0).astype(BF16)
    qbd = (jnp.where(own, jnp.concatenate([q_ref[0]] * SB_HEADS, axis=0), 0.0) * SB_SCALE).astype(BF16)
    pad = jnp.zeros((page - tq, MIX_HALF), F32)
    kn = jnp.concatenate([kn_ref[0], pad], axis=0).astype(BF16)
    vn = jnp.concatenate([vn_ref[0], pad], axis=0).astype(BF16)
    qpos = lax.broadcasted_iota(jnp.int32, (nrow, page), 0) % tq
    kpos = lax.broadcasted_iota(jnp.int32, (nrow, page), 1)
    new_mask = kpos < qpos

    ls0, lk0 = _log_sigmoid_pair(_dot_nt(qbd, kn) + bias)
    ls_all, lk_all = [ls0], [jnp.where(new_mask, lk0, 0.0)]
    for j in range(n_pages):
        ls, lk = _log_sigmoid_pair(_dot(qbd, kc_refs[j][0, 0].astype(BF16)) + bias)
        ls_all.append(ls)
        lk_all.append(lk)
    lk_cat = jnp.concatenate(lk_all, axis=0)
    suffix = _split_dot(lk_cat, upper, 2)
    total = jnp.sum(lk_cat, axis=-1, keepdims=True)
    carry = jnp.zeros((nrow, 1), F32)
    acc = jnp.zeros((nrow, MIX_HALF), F32)
    for j in range(n_pages + 1):
        r = slice(j * nrow, (j + 1) * nrow)
        w = jnp.exp(ls_all[j] + suffix[r] + carry)
        if j == 0:
            acc = acc + _dot(jnp.where(new_mask, w, 0.0).astype(BF16), vn)
        else:
            acc = acc + _dot_nt(w.astype(BF16), vc_refs[j - 1][0, 0].astype(BF16))
        carry = carry + total[r]
    masked = jnp.where(own, acc, 0.0)
    out = masked[0:tq, :]
    for h in range(1, SB_HEADS):
        out = out + masked[h * tq:(h + 1) * tq, :]
    o_ref[0] = out


def _sb_sample(layer, page_table, bias_rows, q, k_new, v_new, cache_k, cache_v):
    bx, tq, _ = q.shape
    n_pages = page_table.shape[1]
    page = cache_k.shape[3]
    nrow = SB_HEADS * tq
    seq = pl.BlockSpec((1, tq, MIX_HALF), lambda b, pt: (b, 0, 0))

    def cache(j):
        return pl.BlockSpec((1, 1, MIX_HALF, page),
                            lambda b, pt: (layer, pt[b * n_pages + (n_pages - 1 - j)], 0, 0))

    pages = [cache(j) for j in range(n_pages)]
    return pl.pallas_call(
        functools.partial(_sb_sample_kernel, tq=tq, page=page, n_pages=n_pages),
        grid_spec=pltpu.PrefetchScalarGridSpec(
            num_scalar_prefetch=1,
            grid=(bx,),
            in_specs=[seq, seq, seq, pl.BlockSpec((nrow, 1), lambda b, pt: (0, 0))] + pages + pages,
            out_specs=seq,
        ),
        out_shape=jax.ShapeDtypeStruct((bx, tq, MIX_HALF), F32),
        compiler_params=_cparams("arbitrary"),
        name="sb_sample",
    )(page_table.reshape(-1), q, k_new, v_new, bias_rows, *([cache_k] * n_pages), *([cache_v] * n_pages))


def _odd_kernel(u_ref, a_ref, gt_ref, hu_ref, hg_ref, pw_ref, ps_ref, cw_ref, cb_ref, lg_ref, lbias_ref,
                po_ref, co_ref, nu_ref, ng_ref, eu_scr, eg_scr, sh_scr, *, tm, q_start):
    t = pl.program_id(1)
    hu0 = 2 * SUBLANES
    hg0 = 4 * SUBLANES

    @pl.when(t == 0)
    def _():
        eu_scr[0:hu0 - POOL_HIST, :] = jnp.zeros((hu0 - POOL_HIST, MIX_HALF), F32)
        eg_scr[0:hg0 - CONV_HIST, :] = jnp.zeros((hg0 - CONV_HIST, MIX_HALF), F32)
        eu_scr[hu0 - POOL_HIST:hu0, :] = hu_ref[0]
        eg_scr[hg0 - CONV_HIST:hg0, :] = hg_ref[0]

    @pl.when(t > 0)
    def _():
        eu_scr[0:hu0, :] = eu_scr[tm:tm + hu0, :]
        eg_scr[0:hg0, :] = eg_scr[tm:tm + hg0, :]

    u = u_ref[0]
    gt = gt_ref[0]
    glu = a_ref[0] * _sigmoid(gt)
    eu_scr[hu0:hu0 + tm, :] = u
    eg_scr[hg0:hg0 + tm, :] = glu

    pos = q_start + t * tm + lax.broadcasted_iota(jnp.int32, (tm, 1), 0)
    for gi, w in enumerate(POOL_WINDOWS):
        c0 = gi * POOL_GC
        wsum = u[:, c0:c0 + POOL_GC]
        for d in range(1, w):
            wsum = wsum + eu_scr[hu0 - d:hu0 - d + tm, c0:c0 + POOL_GC]
        cnt = jnp.minimum(w, pos + 1).astype(F32)
        pooled = wsum / cnt - u[:, c0:c0 + POOL_GC]
        po = _dot(pooled.astype(BF16), pw_ref[gi].astype(BF16))
        po_ref[0, :, c0:c0 + POOL_GC] = po * ps_ref[:, c0:c0 + POOL_GC]

    cv = jnp.broadcast_to(cb_ref[...], (tm, MIX_HALF))
    first = hg0 - CONV_HIST
    for r in range(SUBLANES):
        taps = [j for j in range(CONV_WIDTH) if (first + j) % SUBLANES == r]
        if not taps:
            continue
        span = (first + taps[-1]) // SUBLANES * SUBLANES
        if r == 0:
            src = eg_scr
        else:
            sh_scr[0:span + tm, :] = eg_scr[r:r + span + tm, :]
            src = sh_scr
        for j in taps:
            a0 = first + j - r
            cv = cv + cw_ref[j:j + 1, :] * src[a0:a0 + tm, :]
    mu = jnp.mean(cv, axis=-1, keepdims=True)
    dv = cv - mu
    var = jnp.mean(dv * dv, axis=-1, keepdims=True)
    y = dv * lax.rsqrt(var + EPS) * lg_ref[...] + lbias_ref[...]
    co_ref[0] = y * _sigmoid(y)

    @pl.when(t == pl.num_programs(1) - 1)
    def _():
        nu_ref[0] = eu_scr[hu0 + tm - POOL_HIST:hu0 + tm, :]
        ng_ref[0] = eg_scr[hg0 + tm - CONV_HIST:hg0 + tm, :]


def _odd(u, a, gt, hist_u, hist_g, io, pool_w, pool_scale, conv_w, conv_b, ln_g, ln_b, q_start):
    bx, tx, _ = u.shape
    tm = min(tx, 256)
    tile = pl.BlockSpec((1, tm, MIX_HALF), lambda b, t: (b, t, 0))
    vec = pl.BlockSpec((1, MIX_HALF), lambda b, t: (0, 0))
    hu_spec = pl.BlockSpec((None, 1, POOL_HIST, MIX_HALF), lambda b, t: (io if hist_u.shape[0] > 1 else 0, b, 0, 0))
    hg_spec = pl.BlockSpec((None, 1, CONV_HIST, MIX_HALF), lambda b, t: (io if hist_g.shape[0] > 1 else 0, b, 0, 0))
    return pl.pallas_call(
        functools.partial(_odd_kernel, tm=tm, q_start=q_start),
        grid=(bx, tx // tm),
        in_specs=[tile, tile, tile, hu_spec, hg_spec,
                  pl.BlockSpec((None, len(POOL_WINDOWS), POOL_GC, POOL_GC), lambda b, t: (io, 0, 0, 0)),
                  vec,
                  pl.BlockSpec((None, CONV_WIDTH, MIX_HALF), lambda b, t: (io, 0, 0)),
                  vec, vec, vec],
        out_specs=[tile, tile,
                   pl.BlockSpec((1, POOL_HIST, MIX_HALF), lambda b, t: (b, 0, 0)),
                   pl.BlockSpec((1, CONV_HIST, MIX_HALF), lambda b, t: (b, 0, 0))],
        out_shape=[jax.ShapeDtypeStruct((bx, tx, MIX_HALF), F32), jax.ShapeDtypeStruct((bx, tx, MIX_HALF), F32),
                   jax.ShapeDtypeStruct((bx, POOL_HIST, MIX_HALF), F32),
                   jax.ShapeDtypeStruct((bx, CONV_HIST, MIX_HALF), F32)],
        scratch_shapes=[pltpu.VMEM((2 * SUBLANES + tm, MIX_HALF), F32),
                        pltpu.VMEM((4 * SUBLANES + tm, MIX_HALF), F32),
                        pltpu.VMEM((4 * SUBLANES + tm, MIX_HALF), F32)],
        compiler_params=_cparams("arbitrary", "arbitrary"),
        name="odd_mixer",
    )(u, a, gt, hist_u, hist_g, pool_w, pool_scale[io:io + 1], conv_w, conv_b[io:io + 1], ln_g[io:io + 1],
      ln_b[io:io + 1])


def _post_kernel(a_ref, b_ref, x_ref, g1_ref, sh2_ref, sc2_ref, g2_ref, ng_ref, woa_ref, wob_ref,
                 w1_ref, w2_ref, o_ref, *, tm):
    def rows_of(ref, rows):
        return ref[0] if ref.shape[1] == 1 else ref[0, rows, :]

    for r0 in range(0, tm, POST_GROUP):
        rows = slice(r0, r0 + POST_GROUP)
        m = _dot(a_ref[0, rows, :].astype(BF16), woa_ref[...]) + _dot(b_ref[0, rows, :].astype(BF16), wob_ref[...])
        x1 = x_ref[0, rows, :] + rows_of(g1_ref, rows) * _rms(m, ng_ref[1:2, :])
        h2 = (_rms(x1, ng_ref[2:3, :]) * (1.0 + rows_of(sc2_ref, rows)) + rows_of(sh2_ref, rows)).astype(BF16)
        acc = None
        for c0 in range(0, D_FF, FF_TILE):
            hc = jnp.maximum(_dot(h2, w1_ref[:, c0:c0 + FF_TILE]), 0.0)
            d = _dot((hc * hc).astype(BF16), w2_ref[c0:c0 + FF_TILE, :])
            acc = d if acc is None else acc + d
        o_ref[0, rows, :] = x1 + rows_of(g2_ref, rows) * _rms(acc, ng_ref[3:4, :])


def _post(a, b, x, mod, norm_g, layer, wo_bf16, wo_layer, w1_bf16, w2_bf16):
    bx, tx, _ = x.shape
    tm = min(tx, ROW_TILE)
    half = pl.BlockSpec((1, tm, MIX_HALF), lambda b_, t: (b_, t, 0))
    full = pl.BlockSpec((1, tm, D_MODEL), lambda b_, t: (b_, t, 0))
    once = pl.Buffered(1)
    return pl.pallas_call(
        functools.partial(_post_kernel, tm=tm),
        grid=(bx, tx // tm),
        in_specs=[half, half, full,
                  _mod_spec(mod, tm, 2), _mod_spec(mod, tm, 3), _mod_spec(mod, tm, 4), _mod_spec(mod, tm, 5),
                  pl.BlockSpec((None, 4, D_MODEL), lambda b_, t: (layer, 0, 0)),
                  pl.BlockSpec((None, MIX_HALF, D_MODEL), lambda b_, t: (wo_layer, 0, 0), pipeline_mode=once),
                  pl.BlockSpec((None, MIX_HALF, D_MODEL), lambda b_, t: (wo_layer, 1, 0), pipeline_mode=once),
                  pl.BlockSpec((None, D_MODEL, D_FF), lambda b_, t: (layer, 0, 0), pipeline_mode=once),
                  pl.BlockSpec((None, D_FF, D_MODEL), lambda b_, t: (layer, 0, 0), pipeline_mode=once)],
        out_specs=full,
        out_shape=jax.ShapeDtypeStruct((bx, tx, D_MODEL), F32),
        compiler_params=_cparams("arbitrary", "arbitrary"),
        name="post",
    )(a, b, x, mod, mod, mod, mod, norm_g, wo_bf16, wo_bf16, w1_bf16, w2_bf16)


def kernel(x_prompt, x_sample, c_prompt, c_sample, cache_sb_k, cache_sb_v, state_hgrn, state_pool, state_conv,
           page_table, norm_g, ada_w, ada_b, even_w_in, even_w_out, sb_bias, hgrn_lb_logits, hgrn_norm_g,
           odd_w_in, odd_w_out, pool_w, pool_scale, conv_w, conv_b, conv_ln_g, conv_ln_b, mlp_w1, mlp_w2):
    n_even = even_w_in.shape[0]
    assert n_even == 2 and DEPTH == 4
    bp, tp, _ = x_prompt.shape
    bs, ts, _ = x_sample.shape
    n_pool, page = cache_sb_k.shape[1], cache_sb_k.shape[2]
    past_len = page_table.shape[1] * page
    half = MIX_HALF

    lb_cum = jnp.cumsum(jax.nn.softmax(hgrn_lb_logits.astype(F32), axis=0), axis=0)
    lower_bounds = jnp.maximum(lb_cum - lb_cum[:1], 0.0)

    mod_all = _ada(jnp.concatenate([c_prompt, c_sample], axis=0), ada_w, ada_b)
    even_w_in_b = even_w_in.astype(BF16)
    even_w_out_b = even_w_out.astype(BF16)
    odd_w_in_b = odd_w_in.astype(BF16)
    odd_w_out_b = odd_w_out.astype(BF16)
    w1_b = mlp_w1.astype(BF16)
    w2_b = mlp_w2.astype(BF16)
    cache_k = cache_sb_k.transpose(0, 1, 3, 4, 2).reshape(n_even, n_pool, half, page)
    cache_v = cache_sb_v.transpose(0, 1, 3, 4, 2).reshape(n_even, n_pool, half, page)
    gn = hgrn_norm_g.reshape(n_even, half)

    def run(x, mods, is_sample):
        bx, tx, _ = x.shape
        s_list, k_list, v_list, p_list, c_list = [], [], [], [], []
        for l in range(DEPTH):
            mod = mods[l]
            if l % 2 == 0:
                ie = l // 2
                if is_sample:
                    qfig, q_b, k_b, v_b = _in_proj(
                        x, mod, norm_g, l, even_w_in_b, ie,
                        [(0, 4 * half, "rows"), (4 * half, half, "rows"), (5 * half, half, "rows"),
                         (6 * half, half, "rows")])
                    o_a, s_new = _hgrn(qfig.reshape(bs, ts, 4 * half), lower_bounds[ie:ie + 1], gn[ie:ie + 1],
                                       state_hgrn, ie, s_prev=s_list[0] if ie == 1 else None)
                    bias_rows = jnp.repeat(sb_bias[ie].astype(F32), ts).reshape(SB_HEADS * ts, 1)
                    k_b = k_b.reshape(bs, ts, half)
                    v_b = v_b.reshape(bs, ts, half)
                    o_b = _sb_sample(ie, page_table, bias_rows, q_b.reshape(bs, ts, half), k_b, v_b, cache_k, cache_v)
                    k_list.append(k_b.reshape(bs, ts, SB_HEADS, SB_HD))
                    v_list.append(v_b.reshape(bs, ts, SB_HEADS, SB_HD))
                else:
                    kv_kind = "cols" if ie == 0 else "cols_after"
                    qfig, q_b, k_t, k_tb, v_t, v_tb = _in_proj(
                        x, mod, norm_g, l, even_w_in_b, ie,
                        [(0, 4 * half, "rows"), (4 * half, half, "rows_q"), (5 * half, half, kv_kind),
                         (5 * half, half, "cols_bf16"), (6 * half, half, kv_kind), (6 * half, half, "cols_bf16")],
                        prev=() if ie == 0 else (k_list[0], v_list[0]))
                    o_a, s_new = _hgrn(qfig, lower_bounds[ie:ie + 1], gn[ie:ie + 1],
                                       jnp.zeros((1, bx, HG_HEADS, HG_D, HG_D), F32))
                    o_b = _sb_prompt(sb_bias[ie].astype(F32), q_b, k_tb, v_tb)
                    k_list.append(k_t)
                    v_list.append(v_t)
                mix_a, mix_b, w_out, wo_layer = o_a.reshape(bx, tx, half), o_b.reshape(bx, tx, half), even_w_out_b, ie
                s_list.append(s_new)
            else:
                io = l // 2
                u, a, gt = _in_proj(x, mod, norm_g, l, odd_w_in_b, io,
                                    [(0, half, "rows"), (half, half, "rows"), (2 * half, half, "rows")])
                if is_sample:
                    hb, ht = bs, ts
                    hist_u, hist_g, q_start = state_pool, state_conv, past_len
                else:
                    hb, ht = bx, tx
                    hist_u = jnp.zeros((1, bx, POOL_HIST, half), F32)
                    hist_g = jnp.zeros((1, bx, CONV_HIST, half), F32)
                    q_start = 0
                po, co, new_u, new_g = _odd(u.reshape(hb, ht, half), a.reshape(hb, ht, half),
                                            gt.reshape(hb, ht, half), hist_u, hist_g, io, pool_w, pool_scale,
                                            conv_w, conv_b, conv_ln_g, conv_ln_b, q_start)
                mix_a, mix_b, w_out, wo_layer = po.reshape(bx, tx, half), co.reshape(bx, tx, half), odd_w_out_b, io
                p_list.append(new_u)
                c_list.append(new_g)
            x = _post(mix_a, mix_b, x, mod, norm_g, l, w_out, wo_layer, w1_b, w2_b)
        if is_sample:
            s_all, k_all, v_all = s_list[1], jnp.stack(k_list), jnp.stack(v_list)
        else:
            s_all = jnp.stack(s_list)
            k_all = k_list[1].reshape(2, bx, SB_HEADS, SB_HD, tx).transpose(0, 1, 4, 2, 3)
            v_all = v_list[1].reshape(2, bx, SB_HEADS, SB_HD, tx).transpose(0, 1, 4, 2, 3)
        return (x, s_all, k_all, v_all, jnp.stack(p_list), jnp.stack(c_list))

    mods_p = [mod_all[l, :bp].reshape(bp, 1, 6 * D_MODEL) for l in range(DEPTH)]
    mods_s = [jnp.repeat(mod_all[l, bp:], ts, axis=0).reshape(1, bs * ts, 6 * D_MODEL) for l in range(DEPTH)]
    y_p, s_p, k_p, v_p, pool_p, conv_p = run(x_prompt, mods_p, False)
    y_s, s_s, k_s, v_s, pool_s, conv_s = run(x_sample.reshape(1, bs * ts, D_MODEL), mods_s, True)
    return (y_p, y_s.reshape(bs, ts, D_MODEL), k_p, v_p, k_s, v_s, s_p, s_s, pool_p, pool_s, conv_p, conv_s)
```

```python
import functools

import jax
import jax.numpy as jnp
from jax import lax
from jax.experimental import pallas as pl
from jax.experimental.pallas import tpu as pltpu

F32 = jnp.float32
BF16 = jnp.bfloat16

D_MODEL = 1024
DEPTH = 4
MIX_HALF = D_MODEL // 2
HG_HEADS = 4
HG_D = MIX_HALF // HG_HEADS
SB_HEADS = 8
SB_HD = MIX_HALF // SB_HEADS
SB_SCALE = SB_HD ** -0.5
POOL_WINDOWS = (2, 4, 8, 16)
POOL_GC = MIX_HALF // len(POOL_WINDOWS)
POOL_HIST = max(POOL_WINDOWS) - 1
CONV_WIDTH = 31
CONV_HIST = CONV_WIDTH - 1
D_FF = 4 * D_MODEL
EPS = 1e-6

LANES = 128
SUBLANES = 8
VMEM_LIMIT = 52 * 1024 * 1024

ROW_TILE = 512
FF_TILE = 1024
HG_CHUNK = 128
HG_SUB = 16
HG_SAFE_DECAY = 72.0
HG_DIRECT_SUB = 16
HG_SEQS = 4
ODD_SEQS = 4
SB_BLOCK = 256
SB_PER_TRIP = 2
SB_PAIRS = 2


def _cparams(*sem):
    return pltpu.CompilerParams(dimension_semantics=sem, vmem_limit_bytes=VMEM_LIMIT)


def _dot(a, b):
    return jnp.dot(a, b, preferred_element_type=F32)


def _dot_nt(a, b):
    return lax.dot_general(a, b, (((1,), (1,)), ((), ())), preferred_element_type=F32)


def _split_dot(x, m_bf16, terms):
    acc = None
    r = x
    for i in range(terms):
        p = r.astype(BF16)
        d = _dot(p, m_bf16)
        acc = d if acc is None else acc + d
        if i + 1 < terms:
            r = r - p.astype(F32)
    return acc


def _split_dot_left(m_bf16, x, terms):
    acc = None
    r = x
    for i in range(terms):
        p = r.astype(BF16)
        d = _dot(m_bf16, p)
        acc = d if acc is None else acc + d
        if i + 1 < terms:
            r = r - p.astype(F32)
    return acc


def _sigmoid(x):
    return 1.0 / (1.0 + jnp.exp(-x))


def _log_sigmoid(x):
    return jnp.minimum(x, 0.0) - jnp.log1p(jnp.exp(-jnp.abs(x)))


def _log_sigmoid_pair(z):
    ls = jnp.minimum(z, 0.0) - jnp.log(1.0 + jnp.exp(-jnp.abs(z)))
    return ls, ls - z


def _rms(x, g):
    return x * lax.rsqrt(jnp.mean(x * x, axis=-1, keepdims=True) + EPS) * g


def _ada_kernel(c_ref, w_ref, b_ref, o_ref):
    c = c_ref[...]
    s = (c * _sigmoid(c)).astype(BF16)
    o_ref[0] = _dot(s, w_ref[0].astype(BF16)) + b_ref[0]


def _ada(c_all, ada_w, ada_b):
    nb = c_all.shape[0]
    tn = 1536
    return pl.pallas_call(
        _ada_kernel,
        grid=(DEPTH, 6 * D_MODEL // tn),
        in_specs=[
            pl.BlockSpec((nb, D_MODEL), lambda l, j: (0, 0)),
            pl.BlockSpec((1, D_MODEL, tn), lambda l, j: (l, 0, j)),
            pl.BlockSpec((1, 1, tn), lambda l, j: (l, 0, j)),
        ],
        out_specs=pl.BlockSpec((1, nb, tn), lambda l, j: (l, 0, j)),
        out_shape=jax.ShapeDtypeStruct((DEPTH, nb, 6 * D_MODEL), F32),
        compiler_params=_cparams("arbitrary", "arbitrary"),
        name="ada",
    )(c_all, ada_w, ada_b.reshape(DEPTH, 1, 6 * D_MODEL))


def _mod_spec(mod, tm, chunk):
    if mod.shape[1] == 1:
        return pl.BlockSpec((1, 1, D_MODEL), lambda b, t, *_: (b, 0, chunk))
    return pl.BlockSpec((1, tm, D_MODEL), lambda b, t, *_: (b, t, chunk))


def _inproj_kernel(x_ref, sh_ref, sc_ref, g_ref, w_ref, *rest, plan):
    n_prev = sum(kind == "cols_after" for _, _, kind in plan)
    prev_refs, o_refs = list(rest[:n_prev]), rest[n_prev:]
    h = _rms(x_ref[0], g_ref[0:1, :]) * (1.0 + sc_ref[0]) + sh_ref[0]
    y = _dot(h.astype(BF16), w_ref[...])
    transposed = {}
    for o_ref, (c0, n, kind) in zip(o_refs, plan):
        blk = y[:, c0:c0 + n]
        if kind == "rows":
            o_ref[0] = blk
        elif kind == "rows_q":
            o_ref[0] = (blk * SB_SCALE).astype(BF16)
        else:
            if c0 not in transposed:
                transposed[c0] = blk.T
            if kind == "cols_after":
                o_ref[0, 0] = prev_refs.pop(0)[0]
                o_ref[1, 0] = transposed[c0]
            else:
                o_ref[0] = transposed[c0].astype(o_ref.dtype)


def _in_proj(x, mod, norm_g, layer, w_bf16, w_layer, plan, prev=()):
    bx, tx, _ = x.shape
    tm = min(tx, ROW_TILE)
    n = w_bf16.shape[2]
    out_specs, out_shape, prev_specs = [], [], []
    for _, wd, kind in plan:
        if kind in ("rows", "rows_q"):
            out_specs.append(pl.BlockSpec((1, tm, wd), lambda b, t: (b, t, 0)))
            out_shape.append(jax.ShapeDtypeStruct((bx, tx, wd), F32 if kind == "rows" else BF16))
        elif kind == "cols_after":
            prev_specs.append(pl.BlockSpec((1, wd, tm), lambda b, t: (b, 0, t)))
            out_specs.append(pl.BlockSpec((2, 1, wd, tm), lambda b, t: (0, b, 0, t)))
            out_shape.append(jax.ShapeDtypeStruct((2, bx, wd, tx), F32))
        else:
            out_specs.append(pl.BlockSpec((1, wd, tm), lambda b, t: (b, 0, t)))
            out_shape.append(jax.ShapeDtypeStruct((bx, wd, tx), F32 if kind == "cols" else BF16))
    assert len(prev_specs) == len(prev)
    return pl.pallas_call(
        functools.partial(_inproj_kernel, plan=tuple(plan)),
        grid=(bx, tx // tm),
        in_specs=[
            pl.BlockSpec((1, tm, D_MODEL), lambda b, t: (b, t, 0)),
            _mod_spec(mod, tm, 0),
            _mod_spec(mod, tm, 1),
            pl.BlockSpec((None, 4, D_MODEL), lambda b, t: (layer, 0, 0)),
            pl.BlockSpec((None, D_MODEL, n), lambda b, t: (w_layer, 0, 0)),
        ] + prev_specs,
        out_specs=out_specs,
        out_shape=out_shape,
        compiler_params=_cparams("arbitrary", "arbitrary"),
        name="in_proj",
    )(x, mod, mod, norm_g, w_bf16, *prev)


def _hgrn_direct(s, h, q_ref, i_ref, b_scr, kk_scr, oi_scr, o_scr, *, chunk):
    sub = min(chunk, HG_DIRECT_SUB)
    sl = slice(h * HG_D, (h + 1) * HG_D)
    rows = lax.broadcasted_iota(jnp.int32, (chunk, 1), 0)
    srow = lax.broadcasted_iota(jnp.int32, (sub, 1), 0)

    def sub_block(i, carry):
        r0 = pl.multiple_of(i * sub, sub)
        b_i = b_scr[s, pl.ds(r0, sub), sl]
        q_i = q_ref[s, pl.ds(r0, sub), sl]
        kk_i = kk_scr[s, pl.ds(r0, sub), sl]
        v_i = i_ref[s, pl.ds(r0, sub), sl]
        o = oi_scr[s, pl.ds(r0, sub), sl]
        if chunk > sub:
            anchor = b_i[0:1, :]
            qt = q_i * jnp.exp(b_i - anchor)
            kt = jnp.where(rows < r0, kk_scr[s, :, sl] * jnp.exp(jnp.minimum(anchor - b_scr[s, :, sl], 0.0)), 0.0)
            a_off = _dot_nt(qt.astype(BF16), kt.astype(BF16))
            o = o + _dot(a_off.astype(BF16), i_ref[s, :, sl].astype(BF16))
        o_diag = jnp.zeros((sub, HG_D), F32)
        for tt in range(sub):
            e = jnp.exp(jnp.minimum(b_i[tt:tt + 1, :] - b_i, 0.0))
            d = jnp.sum(q_i[tt:tt + 1, :] * e * kk_i, axis=-1, keepdims=True)
            d = jnp.where(srow <= tt, d, 0.0)
            row = jnp.sum(d * v_i, axis=0, keepdims=True)
            o_diag = jnp.where(srow == tt, row, o_diag)
        o_scr[s, pl.ds(r0, sub), sl] = o + o_diag
        return carry

    lax.fori_loop(0, chunk // sub, sub_block, 0)


def _hgrn_kernel(q_ref, f_ref, i_ref, g_ref, lb_ref, gn_ref, s0_ref, *rest,
                 nb, chunk, sub, single, stacked, safe_decay):
    if stacked:
        sprev_ref, o_ref, sout2_ref, st_scr, b_scr, kk_scr, oi_scr, o_scr = rest
        sout2_ref[0] = sprev_ref[...]
        sout_ref = sout2_ref.at[1]
    else:
        o_ref, sout_ref, st_scr, b_scr, kk_scr, oi_scr, o_scr = rest
    t = pl.program_id(1)
    if not single:
        @pl.when(t == 0)
        def _():
            for s in range(nb):
                for h in range(HG_HEADS):
                    st_scr[s * HG_HEADS + h] = s0_ref[s, h].T

    worst = [_hgrn_main(s, q_ref, f_ref, i_ref, lb_ref, s0_ref, sout_ref, st_scr, b_scr, kk_scr, oi_scr, o_scr,
                        chunk=chunk, sub=sub, single=single, safe_decay=safe_decay) for s in range(nb)]

    for s in range(nb):
        @pl.when(jnp.max(worst[s]) > safe_decay)
        def _(s=s):
            for h in range(HG_HEADS):
                _hgrn_direct(s, h, q_ref, i_ref, b_scr, kk_scr, oi_scr, o_scr, chunk=chunk)

    gn = gn_ref[...]
    for s in range(nb):
        g = g_ref[s]
        for h in range(HG_HEADS):
            sl = slice(h * HG_D, (h + 1) * HG_D)
            gh = g[:, sl]
            o_ref[s, :, sl] = _rms(o_scr[s, :, sl], gn[:, sl]) * (gh * _sigmoid(gh))

    if not single:
        @pl.when(t == pl.num_programs(1) - 1)
        def _():
            for s in range(nb):
                for h in range(HG_HEADS):
                    sout_ref[s, h] = st_scr[s * HG_HEADS + h].T


def _hgrn_main(s, q_ref, f_ref, i_ref, lb_ref, s0_ref, sout_ref, st_scr, b_scr, kk_scr, oi_scr, o_scr,
               *, chunk, sub, single, safe_decay):
    lb = lb_ref[...]
    q = q_ref[s]
    fl = f_ref[s]
    v = i_ref[s]
    e = jnp.exp(-jnp.abs(fl))
    d = 1.0 + e
    r = 1.0 / d
    c = jnp.log1p(-lb) + (jnp.minimum(fl, 0.0) - jnp.log(d))
    a = jnp.log(lb)
    log_f = jnp.maximum(a, c) + jnp.log(1.0 + jnp.exp(-jnp.abs(a - c)))
    kk = (1.0 - lb) * jnp.where(fl >= 0.0, e * r, r)
    ri = lax.broadcasted_iota(jnp.int32, (chunk, chunk), 0)
    ci = lax.broadcasted_iota(jnp.int32, (chunk, chunk), 1)
    tril = jnp.where(ri >= ci, 1.0, 0.0).astype(BF16)
    p1 = log_f.astype(BF16)
    r1 = log_f - p1.astype(F32)
    p2 = r1.astype(BF16)
    p3 = (r1 - p2.astype(F32)).astype(BF16)
    b3 = _dot(tril, jnp.concatenate([p1, p2, p3], axis=1))
    b = b3[:, 0:MIX_HALF] + b3[:, MIX_HALF:2 * MIX_HALF] + b3[:, 2 * MIX_HALF:3 * MIX_HALF]
    b_scr[s] = b
    kk_scr[s] = kk
    b_last = b[chunk - 1:chunk, :]
    qe = (q * jnp.exp(b)).astype(BF16)
    kd = kk * jnp.exp(b_last - b)
    dec = jnp.exp(b_last)
    npad = HG_D - chunk
    prow = lax.broadcasted_iota(jnp.int32, (HG_D, 1), 0)
    for h in range(HG_HEADS):
        sl = slice(h * HG_D, (h + 1) * HG_D)
        if single:
            st = s0_ref[s, h]
            oi_scr[s, :, sl] = _dot(qe[:, sl], st.astype(BF16))
            zpad = jnp.zeros((npad, HG_D), F32)
            m = jnp.concatenate([kd[:, sl], zpad], axis=0)
            m = jnp.where(prow == chunk, dec[:, sl], m)
            mt = m.T
            vpad = jnp.concatenate([v[:, sl], zpad], axis=0)
            sout_ref[s, h] = mt[:, chunk:chunk + 1] * st + _dot(mt.astype(BF16), vpad.astype(BF16))
        else:
            st = st_scr[s * HG_HEADS + h]
            oi_scr[s, :, sl] = _dot_nt(qe[:, sl], st.astype(BF16))
            st_scr[s * HG_HEADS + h] = dec[:, sl] * st + _dot(v[:, sl].T.astype(BF16), kd[:, sl].astype(BF16))

    n_sub = chunk // sub
    worst = jnp.zeros((1, MIX_HALF), F32)
    if single:
        zpad = jnp.zeros((npad, MIX_HALF), F32)
        kk_full = jnp.concatenate([kk, zpad], axis=0)
        b_full = jnp.concatenate([b, zpad], axis=0)
        v_full = jnp.concatenate([v, zpad], axis=0)
    else:
        kk_full, b_full, v_full = kk, b, v
    anchors = [b[i * sub:i * sub + 1, :] for i in range(n_sub)]
    for i in range(n_sub):
        worst = jnp.maximum(worst, anchors[i] - b[(i + 1) * sub - 1:(i + 1) * sub, :])
    anchor_rows = jnp.concatenate([jnp.broadcast_to(an, (sub, MIX_HALF)) for an in anchors], axis=0)
    qt = (q * jnp.exp(b - anchor_rows)).astype(BF16)
    kts = []
    for i, an in enumerate(anchors):
        n = kk_full.shape[0] if single else (i + 1) * sub
        kt = (kk_full[0:n] * jnp.exp(jnp.minimum(an - b_full[0:n], safe_decay))).astype(BF16)
        if n < kk_full.shape[0]:
            kt = jnp.concatenate([kt, jnp.zeros((kk_full.shape[0] - n, MIX_HALF), BF16)], axis=0)
        kts.append(kt)
    row_blk = lax.broadcasted_iota(jnp.int32, (chunk, HG_D), 0) // sub
    causal = (lax.broadcasted_iota(jnp.int32, (chunk, HG_D), 1)
              <= lax.broadcasted_iota(jnp.int32, (chunk, HG_D), 0))
    zero_q = jnp.zeros((chunk, HG_D), BF16)
    atts = []
    for h in range(HG_HEADS):
        sl = slice(h * HG_D, (h + 1) * HG_D)
        q_cat = jnp.concatenate([jnp.where(row_blk == i, qt[:, sl], zero_q) for i in range(n_sub)], axis=1)
        k_cat = jnp.concatenate([kt[:, sl] for kt in kts], axis=1)
        atts.append(jnp.where(causal, _dot_nt(q_cat, k_cat), 0.0).astype(BF16))
    bd = (lax.broadcasted_iota(jnp.int32, (HG_HEADS * HG_D, MIX_HALF), 0) // HG_D
          == lax.broadcasted_iota(jnp.int32, (HG_HEADS * HG_D, MIX_HALF), 1) // HG_D)
    v_bd = jnp.where(bd, jnp.concatenate([v_full] * HG_HEADS, axis=0), 0.0).astype(BF16)
    o_scr[s] = oi_scr[s] + _dot(jnp.concatenate(atts, axis=1), v_bd)
    return worst


def _hgrn(qfig, lb_l, gn_l, s0, s0_layer=0, s_prev=None, safe_decay=HG_SAFE_DECAY):
    bx, tx, _ = qfig.shape
    chunk = min(tx, HG_CHUNK)
    sub = min(chunk, HG_SUB)
    single = tx == chunk
    assert not single or chunk < HG_D
    nb = HG_SEQS if bx % HG_SEQS == 0 else 1

    def col(k):
        return pl.BlockSpec((nb, chunk, MIX_HALF), lambda b, t: (b, t, k))

    vec = pl.BlockSpec((1, MIX_HALF), lambda b, t: (0, 0))
    state = pl.BlockSpec((nb, HG_HEADS, HG_D, HG_D), lambda b, t: (b, 0, 0, 0))
    state_in = pl.BlockSpec((None, nb, HG_HEADS, HG_D, HG_D), lambda b, t: (s0_layer, b, 0, 0, 0))
    stacked = s_prev is not None
    if stacked:
        state_out = pl.BlockSpec((2, nb, HG_HEADS, HG_D, HG_D), lambda b, t: (0, b, 0, 0, 0))
        state_shape = jax.ShapeDtypeStruct((2, bx, HG_HEADS, HG_D, HG_D), F32)
    else:
        state_out = state
        state_shape = jax.ShapeDtypeStruct((bx, HG_HEADS, HG_D, HG_D), F32)
    return pl.pallas_call(
        functools.partial(_hgrn_kernel, nb=nb, chunk=chunk, sub=sub, single=single, stacked=stacked,
                          safe_decay=safe_decay),
        grid=(bx // nb, tx // chunk),
        in_specs=[col(0), col(1), col(2), col(3), vec, vec, state_in] + ([state] if stacked else []),
        out_specs=[pl.BlockSpec((nb, chunk, MIX_HALF), lambda b, t: (b, t, 0)), state_out],
        out_shape=[jax.ShapeDtypeStruct((bx, tx, MIX_HALF), F32), state_shape],
        scratch_shapes=([pltpu.VMEM((nb * HG_HEADS, HG_D, HG_D), F32)]
                        + [pltpu.VMEM((nb, chunk, MIX_HALF), F32)] * 4),
        compiler_params=_cparams("arbitrary", "arbitrary"),
        name="hgrn",
    )(qfig, qfig, qfig, qfig, lb_l, gn_l, s0, *([s_prev] if stacked else []))


def _sb_prompt_kernel(bias_ref, q_ref, kt_ref, vt_ref, o_ref, acc_scr, carry_scr, *, blk, per_trip):
    g = pl.program_id(1)
    qi = pl.program_id(2)
    lane = lax.broadcasted_iota(jnp.int32, (1, LANES), 1)
    ri = lax.broadcasted_iota(jnp.int32, (blk, blk), 0)
    ci = lax.broadcasted_iota(jnp.int32, (blk, blk), 1)
    upper = jnp.where(ri > ci, 1.0, 0.0).astype(BF16)
    ones = jnp.ones((blk, LANES), BF16)
    tri = ci < ri
    causal = jnp.concatenate([tri, tri], axis=0)
    second = lax.broadcasted_iota(jnp.int32, (2 * blk, 1), 0) >= blk
    qs, bias = [], []
    for p in range(SB_PAIRS):
        q = q_ref[0, :, p * LANES:(p + 1) * LANES]
        zero = jnp.zeros_like(q)
        qs.append(jnp.concatenate([jnp.where(lane < SB_HD, q, zero), jnp.where(lane >= SB_HD, q, zero)], axis=0))
        h0 = 2 * (g * SB_PAIRS + p)
        bias.append(jnp.where(second, bias_ref[h0 + 1], bias_ref[h0]))
    acc_scr[...] = jnp.zeros_like(acc_scr)
    carry_scr[...] = jnp.zeros_like(carry_scr)

    def step(kbs, masked):
        for p in range(SB_PAIRS):
            carry = carry_scr[p]
            pv = None
            for kb in kbs:
                c0 = pl.multiple_of(kb * blk, blk)
                kblk = kt_ref[0, p * LANES:(p + 1) * LANES, pl.ds(c0, blk)]
                vblk = vt_ref[0, p * LANES:(p + 1) * LANES, pl.ds(c0, blk)]
                ls, lk = _log_sigmoid_pair(_dot(qs[p], kblk) + bias[p])
                if masked:
                    lk = jnp.where(causal, lk, 0.0)
                lkb = lk.astype(BF16)
                w = jnp.exp(ls + _dot(lkb, upper) + jnp.concatenate([carry] * (blk // LANES), axis=1))
                if masked:
                    w = jnp.where(causal, w, 0.0)
                pv_k = _dot_nt(w.astype(BF16), vblk)
                pv = pv_k if pv is None else pv + pv_k
                carry = carry + _dot(lkb, ones)
            acc_scr[p] += pv
            carry_scr[p] = carry

    step([qi], True)

    def body(j, c):
        kb = qi - 1 - per_trip * j
        step([kb - i for i in range(per_trip)], False)
        return c

    lax.fori_loop(0, qi // per_trip, body, 0)
    left = qi % per_trip

    def tail(j, c):
        step([left - 1 - j], False)
        return c

    lax.fori_loop(0, left, tail, 0)
    for p in range(SB_PAIRS):
        o_ref[0, :, p * LANES:(p + 1) * LANES] = jnp.where(lane < SB_HD, acc_scr[p, 0:blk], acc_scr[p, blk:2 * blk])


def _sb_prompt(bias_l, q_bf16, kt_bf16, vt_bf16):
    bx, tx, _ = q_bf16.shape
    blk = SB_BLOCK
    wd = SB_PAIRS * LANES
    rows = pl.BlockSpec((1, blk, wd), lambda b, g, qi: (b, qi, g))
    full = pl.BlockSpec((1, wd, tx), lambda b, g, qi: (b, g, 0))
    return pl.pallas_call(
        functools.partial(_sb_prompt_kernel, blk=blk, per_trip=SB_PER_TRIP),
        grid=(bx, MIX_HALF // wd, tx // blk),
        in_specs=[pl.BlockSpec(memory_space=pltpu.SMEM), rows, full, full],
        out_specs=rows,
        out_shape=jax.ShapeDtypeStruct((bx, tx, MIX_HALF), F32),
        scratch_shapes=[pltpu.VMEM((SB_PAIRS, 2 * blk, LANES), F32), pltpu.VMEM((SB_PAIRS, 2 * blk, LANES), F32)],
        compiler_params=_cparams("arbitrary", "arbitrary", "arbitrary"),
        name="sb_prompt",
    )(bias_l, q_bf16, kt_bf16, vt_bf16)


def _sb_sample_kernel(pt_ref, q_ref, kn_ref, vn_ref, bias_ref, *rest, tq, page, n_pages):
    kc_refs, vc_refs, o_ref = rest[:n_pages], rest[n_pages:2 * n_pages], rest[2 * n_pages]
    nrow = SB_HEADS * tq
    rowh = lax.broadcasted_iota(jnp.int32, (nrow, MIX_HALF), 0) // tq
    colh = lax.broadcasted_iota(jnp.int32, (nrow, MIX_HALF), 1) // SB_HD
    own = rowh == colh
    bias = bias_ref[...]
    ri = lax.broadcasted_iota(jnp.int32, (page, page), 0)
    ci = lax.broadcasted_iota(jnp.int32, (page, page), 1)
    upper = jnp.where(ri > ci, 1.0, 0.0).astype(BF16)
    qbd = (jnp.where(own, jnp.concatenate([q_ref[0]] * SB_HEADS, axis=0), 0.0) * SB_SCALE).astype(BF16)
    pad = jnp.zeros((page - tq, MIX_HALF), F32)
    kn = jnp.concatenate([kn_ref[0], pad], axis=0).astype(BF16)
    vn = jnp.concatenate([vn_ref[0], pad], axis=0).astype(BF16)
    qpos = lax.broadcasted_iota(jnp.int32, (nrow, page), 0) % tq
    kpos = lax.broadcasted_iota(jnp.int32, (nrow, page), 1)
    new_mask = kpos < qpos

    ls0, lk0 = _log_sigmoid_pair(_dot_nt(qbd, kn) + bias)
    ls_all, lk_all = [ls0], [jnp.where(new_mask, lk0, 0.0)]
    for j in range(n_pages):
        ls, lk = _log_sigmoid_pair(_dot(qbd, kc_refs[j][0, 0].astype(BF16)) + bias)
        ls_all.append(ls)
        lk_all.append(lk)
    lk_cat = jnp.concatenate(lk_all, axis=0)
    suffix = _split_dot(lk_cat, upper, 2)
    total = jnp.sum(lk_cat, axis=-1, keepdims=True)
    carry = jnp.zeros((nrow, 1), F32)
    acc = jnp.zeros((nrow, MIX_HALF), F32)
    for j in range(n_pages + 1):
        r = slice(j * nrow, (j + 1) * nrow)
        w = jnp.exp(ls_all[j] + suffix[r] + carry)
        if j == 0:
            acc = acc + _dot(jnp.where(new_mask, w, 0.0).astype(BF16), vn)
        else:
            acc = acc + _dot_nt(w.astype(BF16), vc_refs[j - 1][0, 0].astype(BF16))
        carry = carry + total[r]
    masked = jnp.where(own, acc, 0.0)
    out = masked[0:tq, :]
    for h in range(1, SB_HEADS):
        out = out + masked[h * tq:(h + 1) * tq, :]
    o_ref[0] = out


def _sb_sample(layer, page_table, bias_rows, q, k_new, v_new, cache_k, cache_v):
    bx, tq, _ = q.shape
    n_pages = page_table.shape[1]
    page = cache_k.shape[3]
    nrow = SB_HEADS * tq
    seq = pl.BlockSpec((1, tq, MIX_HALF), lambda b, pt: (b, 0, 0))

    def cache(j):
        return pl.BlockSpec((1, 1, MIX_HALF, page),
                            lambda b, pt: (layer, pt[b * n_pages + (n_pages - 1 - j)], 0, 0))

    pages = [cache(j) for j in range(n_pages)]
    return pl.pallas_call(
        functools.partial(_sb_sample_kernel, tq=tq, page=page, n_pages=n_pages),
        grid_spec=pltpu.PrefetchScalarGridSpec(
            num_scalar_prefetch=1,
            grid=(bx,),
            in_specs=[seq, seq, seq, pl.BlockSpec((nrow, 1), lambda b, pt: (0, 0))] + pages + pages,
            out_specs=seq,
        ),
        out_shape=jax.ShapeDtypeStruct((bx, tq, MIX_HALF), F32),
        compiler_params=_cparams("arbitrary"),
        name="sb_sample",
    )(page_table.reshape(-1), q, k_new, v_new, bias_rows, *([cache_k] * n_pages), *([cache_v] * n_pages))


def _odd_kernel(*refs, nb, tm, q_start):
    for s in range(nb):
        _odd_one(s, *refs, tm=tm, q_start=q_start)


def _odd_one(s, u_ref, a_ref, gt_ref, hu_ref, hg_ref, pw_ref, ps_ref, cw_ref, cb_ref, lg_ref, lbias_ref,
             po_ref, co_ref, nu_ref, ng_ref, eu_scr, eg_scr, sh_scr, *, tm, q_start):
    t = pl.program_id(1)
    hu0 = 2 * SUBLANES
    hg0 = 4 * SUBLANES

    @pl.when(t == 0)
    def _():
        eu_scr[0:hu0 - POOL_HIST, :] = jnp.zeros((hu0 - POOL_HIST, MIX_HALF), F32)
        eg_scr[0:hg0 - CONV_HIST, :] = jnp.zeros((hg0 - CONV_HIST, MIX_HALF), F32)
        eu_scr[hu0 - POOL_HIST:hu0, :] = hu_ref[s]
        eg_scr[hg0 - CONV_HIST:hg0, :] = hg_ref[s]

    @pl.when(t > 0)
    def _():
        eu_scr[0:hu0, :] = eu_scr[tm:tm + hu0, :]
        eg_scr[0:hg0, :] = eg_scr[tm:tm + hg0, :]

    u = u_ref[s]
    gt = gt_ref[s]
    glu = a_ref[s] * _sigmoid(gt)
    eu_scr[hu0:hu0 + tm, :] = u
    eg_scr[hg0:hg0 + tm, :] = glu

    pos = q_start + t * tm + lax.broadcasted_iota(jnp.int32, (tm, 1), 0)
    for gi, w in enumerate(POOL_WINDOWS):
        c0 = gi * POOL_GC
        wsum = u[:, c0:c0 + POOL_GC]
        for d in range(1, w):
            wsum = wsum + eu_scr[hu0 - d:hu0 - d + tm, c0:c0 + POOL_GC]
        cnt = jnp.minimum(w, pos + 1).astype(F32)
        pooled = wsum / cnt - u[:, c0:c0 + POOL_GC]
        po = _dot(pooled.astype(BF16), pw_ref[gi].astype(BF16))
        po_ref[s, :, c0:c0 + POOL_GC] = po * ps_ref[:, c0:c0 + POOL_GC]

    cv = jnp.broadcast_to(cb_ref[...], (tm, MIX_HALF))
    first = hg0 - CONV_HIST
    for r in range(SUBLANES):
        taps = [j for j in range(CONV_WIDTH) if (first + j) % SUBLANES == r]
        if not taps:
            continue
        span = (first + taps[-1]) // SUBLANES * SUBLANES
        if r == 0:
            src = eg_scr
        else:
            sh_scr[0:span + tm, :] = eg_scr[r:r + span + tm, :]
            src = sh_scr
        for j in taps:
            a0 = first + j - r
            cv = cv + cw_ref[j:j + 1, :] * src[a0:a0 + tm, :]
    mu = jnp.mean(cv, axis=-1, keepdims=True)
    dv = cv - mu
    var = jnp.mean(dv * dv, axis=-1, keepdims=True)
    y = dv * lax.rsqrt(var + EPS) * lg_ref[...] + lbias_ref[...]
    co_ref[s] = y * _sigmoid(y)

    @pl.when(t == pl.num_programs(1) - 1)
    def _():
        nu_ref[s] = eu_scr[hu0 + tm - POOL_HIST:hu0 + tm, :]
        ng_ref[s] = eg_scr[hg0 + tm - CONV_HIST:hg0 + tm, :]


def _odd(u, a, gt, hist_u, hist_g, io, pool_w, pool_scale, conv_w, conv_b, ln_g, ln_b, q_start):
    bx, tx, _ = u.shape
    tm = min(tx, 256)
    nb = ODD_SEQS if (tx == tm and bx % ODD_SEQS == 0) else 1
    tile = pl.BlockSpec((nb, tm, MIX_HALF), lambda b, t: (b, t, 0))
    vec = pl.BlockSpec((1, MIX_HALF), lambda b, t: (0, 0))
    hu_spec = pl.BlockSpec((None, nb, POOL_HIST, MIX_HALF), lambda b, t: (io if hist_u.shape[0] > 1 else 0, b, 0, 0))
    hg_spec = pl.BlockSpec((None, nb, CONV_HIST, MIX_HALF), lambda b, t: (io if hist_g.shape[0] > 1 else 0, b, 0, 0))
    return pl.pallas_call(
        functools.partial(_odd_kernel, nb=nb, tm=tm, q_start=q_start),
        grid=(bx // nb, tx // tm),
        in_specs=[tile, tile, tile, hu_spec, hg_spec,
                  pl.BlockSpec((None, len(POOL_WINDOWS), POOL_GC, POOL_GC), lambda b, t: (io, 0, 0, 0)),
                  vec,
                  pl.BlockSpec((None, CONV_WIDTH, MIX_HALF), lambda b, t: (io, 0, 0)),
                  vec, vec, vec],
        out_specs=[tile, tile,
                   pl.BlockSpec((nb, POOL_HIST, MIX_HALF), lambda b, t: (b, 0, 0)),
                   pl.BlockSpec((nb, CONV_HIST, MIX_HALF), lambda b, t: (b, 0, 0))],
        out_shape=[jax.ShapeDtypeStruct((bx, tx, MIX_HALF), F32), jax.ShapeDtypeStruct((bx, tx, MIX_HALF), F32),
                   jax.ShapeDtypeStruct((bx, POOL_HIST, MIX_HALF), F32),
                   jax.ShapeDtypeStruct((bx, CONV_HIST, MIX_HALF), F32)],
        scratch_shapes=[pltpu.VMEM((2 * SUBLANES + tm, MIX_HALF), F32),
                        pltpu.VMEM((4 * SUBLANES + tm, MIX_HALF), F32),
                        pltpu.VMEM((4 * SUBLANES + tm, MIX_HALF), F32)],
        compiler_params=_cparams("arbitrary", "arbitrary"),
        name="odd_mixer",
    )(u, a, gt, hist_u, hist_g, pool_w, pool_scale[io:io + 1], conv_w, conv_b[io:io + 1], ln_g[io:io + 1],
      ln_b[io:io + 1])


def _post_kernel(a_ref, b_ref, x_ref, g1_ref, sh2_ref, sc2_ref, g2_ref, ng_ref, woa_ref, wob_ref,
                 w1_ref, w2_ref, o_ref):
    m = _dot(a_ref[0].astype(BF16), woa_ref[...]) + _dot(b_ref[0].astype(BF16), wob_ref[...])
    x1 = x_ref[0] + g1_ref[0] * _rms(m, ng_ref[1:2, :])
    h2 = (_rms(x1, ng_ref[2:3, :]) * (1.0 + sc2_ref[0]) + sh2_ref[0]).astype(BF16)
    acc = None
    for c0 in range(0, D_FF, FF_TILE):
        hc = jnp.maximum(_dot(h2, w1_ref[:, c0:c0 + FF_TILE]), 0.0)
        d = _dot((hc * hc).astype(BF16), w2_ref[c0:c0 + FF_TILE, :])
        acc = d if acc is None else acc + d
    o_ref[0] = x1 + g2_ref[0] * _rms(acc, ng_ref[3:4, :])


def _post(a, b, x, mod, norm_g, layer, wo_bf16, wo_layer, w1_bf16, w2_bf16):
    bx, tx, _ = x.shape
    tm = min(tx, ROW_TILE)
    half = pl.BlockSpec((1, tm, MIX_HALF), lambda b_, t: (b_, t, 0))
    full = pl.BlockSpec((1, tm, D_MODEL), lambda b_, t: (b_, t, 0))
    once = pl.Buffered(1)
    return pl.pallas_call(
        _post_kernel,
        grid=(bx, tx // tm),
        in_specs=[half, half, full,
                  _mod_spec(mod, tm, 2), _mod_spec(mod, tm, 3), _mod_spec(mod, tm, 4), _mod_spec(mod, tm, 5),
                  pl.BlockSpec((None, 4, D_MODEL), lambda b_, t: (layer, 0, 0)),
                  pl.BlockSpec((None, MIX_HALF, D_MODEL), lambda b_, t: (wo_layer, 0, 0), pipeline_mode=once),
                  pl.BlockSpec((None, MIX_HALF, D_MODEL), lambda b_, t: (wo_layer, 1, 0), pipeline_mode=once),
                  pl.BlockSpec((None, D_MODEL, D_FF), lambda b_, t: (layer, 0, 0), pipeline_mode=once),
                  pl.BlockSpec((None, D_FF, D_MODEL), lambda b_, t: (layer, 0, 0), pipeline_mode=once)],
        out_specs=full,
        out_shape=jax.ShapeDtypeStruct((bx, tx, D_MODEL), F32),
        compiler_params=_cparams("arbitrary", "arbitrary"),
        name="post",
    )(a, b, x, mod, mod, mod, mod, norm_g, wo_bf16, wo_bf16, w1_bf16, w2_bf16)


def kernel(x_prompt, x_sample, c_prompt, c_sample, cache_sb_k, cache_sb_v, state_hgrn, state_pool, state_conv,
           page_table, norm_g, ada_w, ada_b, even_w_in, even_w_out, sb_bias, hgrn_lb_logits, hgrn_norm_g,
           odd_w_in, odd_w_out, pool_w, pool_scale, conv_w, conv_b, conv_ln_g, conv_ln_b, mlp_w1, mlp_w2):
    n_even = even_w_in.shape[0]
    assert n_even == 2 and DEPTH == 4
    bp, tp, _ = x_prompt.shape
    bs, ts, _ = x_sample.shape
    n_pool, page = cache_sb_k.shape[1], cache_sb_k.shape[2]
    past_len = page_table.shape[1] * page
    half = MIX_HALF

    lb_cum = jnp.cumsum(jax.nn.softmax(hgrn_lb_logits.astype(F32), axis=0), axis=0)
    lower_bounds = jnp.maximum(lb_cum - lb_cum[:1], 0.0)

    mod_all = _ada(jnp.concatenate([c_prompt, c_sample], axis=0), ada_w, ada_b)
    even_w_in_b = even_w_in.astype(BF16)
    even_w_out_b = even_w_out.astype(BF16)
    odd_w_in_b = odd_w_in.astype(BF16)
    odd_w_out_b = odd_w_out.astype(BF16)
    w1_b = mlp_w1.astype(BF16)
    w2_b = mlp_w2.astype(BF16)
    cache_k = cache_sb_k.transpose(0, 1, 3, 4, 2).reshape(n_even, n_pool, half, page)
    cache_v = cache_sb_v.transpose(0, 1, 3, 4, 2).reshape(n_even, n_pool, half, page)
    gn = hgrn_norm_g.reshape(n_even, half)

    def run(x, mods, is_sample):
        bx, tx, _ = x.shape
        s_list, k_list, v_list, p_list, c_list = [], [], [], [], []
        for l in range(DEPTH):
            mod = mods[l]
            if l % 2 == 0:
                ie = l // 2
                if is_sample:
                    qfig, q_b, k_b, v_b = _in_proj(
                        x, mod, norm_g, l, even_w_in_b, ie,
                        [(0, 4 * half, "rows"), (4 * half, half, "rows"), (5 * half, half, "rows"),
                         (6 * half, half, "rows")])
                    o_a, s_new = _hgrn(qfig.reshape(bs, ts, 4 * half), lower_bounds[ie:ie + 1], gn[ie:ie + 1],
                                       state_hgrn, ie, s_prev=s_list[0] if ie == 1 else None)
                    bias_rows = jnp.repeat(sb_bias[ie].astype(F32), ts).reshape(SB_HEADS * ts, 1)
                    k_b = k_b.reshape(bs, ts, half)
                    v_b = v_b.reshape(bs, ts, half)
                    o_b = _sb_sample(ie, page_table, bias_rows, q_b.reshape(bs, ts, half), k_b, v_b, cache_k, cache_v)
                    k_list.append(k_b.reshape(bs, ts, SB_HEADS, SB_HD))
                    v_list.append(v_b.reshape(bs, ts, SB_HEADS, SB_HD))
                else:
                    kv_kind = "cols" if ie == 0 else "cols_after"
                    qfig, q_b, k_t, k_tb, v_t, v_tb = _in_proj(
                        x, mod, norm_g, l, even_w_in_b, ie,
                        [(0, 4 * half, "rows"), (4 * half, half, "rows_q"), (5 * half, half, kv_kind),
                         (5 * half, half, "cols_bf16"), (6 * half, half, kv_kind), (6 * half, half, "cols_bf16")],
                        prev=() if ie == 0 else (k_list[0], v_list[0]))
                    o_a, s_new = _hgrn(qfig, lower_bounds[ie:ie + 1], gn[ie:ie + 1],
                                       jnp.zeros((1, bx, HG_HEADS, HG_D, HG_D), F32))
                    o_b = _sb_prompt(sb_bias[ie].astype(F32), q_b, k_tb, v_tb)
                    k_list.append(k_t)
                    v_list.append(v_t)
                mix_a, mix_b, w_out, wo_layer = o_a.reshape(bx, tx, half), o_b.reshape(bx, tx, half), even_w_out_b, ie
                s_list.append(s_new)
            else:
                io = l // 2
                u, a, gt = _in_proj(x, mod, norm_g, l, odd_w_in_b, io,
                                    [(0, half, "rows"), (half, half, "rows"), (2 * half, half, "rows")])
                if is_sample:
                    hb, ht = bs, ts
                    hist_u, hist_g, q_start = state_pool, state_conv, past_len
                else:
                    hb, ht = bx, tx
                    hist_u = jnp.zeros((1, bx, POOL_HIST, half), F32)
                    hist_g = jnp.zeros((1, bx, CONV_HIST, half), F32)
                    q_start = 0
                po, co, new_u, new_g = _odd(u.reshape(hb, ht, half), a.reshape(hb, ht, half),
                                            gt.reshape(hb, ht, half), hist_u, hist_g, io, pool_w, pool_scale,
                                            conv_w, conv_b, conv_ln_g, conv_ln_b, q_start)
                mix_a, mix_b, w_out, wo_layer = po.reshape(bx, tx, half), co.reshape(bx, tx, half), odd_w_out_b, io
                p_list.append(new_u)
                c_list.append(new_g)
            x = _post(mix_a, mix_b, x, mod, norm_g, l, w_out, wo_layer, w1_b, w2_b)
        if is_sample:
            s_all, k_all, v_all = s_list[1], jnp.stack(k_list), jnp.stack(v_list)
        else:
            s_all = jnp.stack(s_list)
            k_all = k_list[1].reshape(2, bx, SB_HEADS, SB_HD, tx).transpose(0, 1, 4, 2, 3)
            v_all = v_list[1].reshape(2, bx, SB_HEADS, SB_HD, tx).transpose(0, 1, 4, 2, 3)
        return (x, s_all, k_all, v_all, jnp.stack(p_list), jnp.stack(c_list))

    mods_p = [mod_all[l, :bp].reshape(bp, 1, 6 * D_MODEL) for l in range(DEPTH)]
    mods_s = [jnp.repeat(mod_all[l, bp:], ts, axis=0).reshape(1, bs * ts, 6 * D_MODEL) for l in range(DEPTH)]
    y_p, s_p, k_p, v_p, pool_p, conv_p = run(x_prompt, mods_p, False)
    y_s, s_s, k_s, v_s, pool_s, conv_s = run(x_sample.reshape(1, bs * ts, D_MODEL), mods_s, True)
    return (y_p, y_s.reshape(bs, ts, D_MODEL), k_p, v_p, k_s, v_s, s_p, s_s, pool_p, pool_s, conv_p, conv_s)
```

```python
import functools

import jax
import jax.numpy as jnp
from jax import lax
from jax.experimental import pallas as pl
from jax.experimental.pallas import tpu as pltpu

F32 = jnp.float32
BF16 = jnp.bfloat16

D_MODEL = 1024
DEPTH = 4
MIX_HALF = D_MODEL // 2
HG_HEADS = 4
HG_D = MIX_HALF // HG_HEADS
SB_HEADS = 8
SB_HD = MIX_HALF // SB_HEADS
SB_SCALE = SB_HD ** -0.5
POOL_WINDOWS = (2, 4, 8, 16)
POOL_GC = MIX_HALF // len(POOL_WINDOWS)
POOL_HIST = max(POOL_WINDOWS) - 1
CONV_WIDTH = 31
CONV_HIST = CONV_WIDTH - 1
D_FF = 4 * D_MODEL
EPS = 1e-6

LANES = 128
SUBLANES = 8
VMEM_LIMIT = 52 * 1024 * 1024

ROW_TILE = 512
FF_TILE = 1024
HG_CHUNK = 128
HG_SUB = 16
HG_SAFE_DECAY = 72.0
HG_DIRECT_SUB = 16
HG_SEQS = 4
ODD_SEQS = 4
SB_BLOCK = 256
SB_PER_TRIP = 2
SB_PAIRS = 2


def _cparams(*sem):
    return pltpu.CompilerParams(dimension_semantics=sem, vmem_limit_bytes=VMEM_LIMIT)


def _dot(a, b):
    return jnp.dot(a, b, preferred_element_type=F32)


def _dot_nt(a, b):
    return lax.dot_general(a, b, (((1,), (1,)), ((), ())), preferred_element_type=F32)


def _split_dot(x, m_bf16, terms):
    acc = None
    r = x
    for i in range(terms):
        p = r.astype(BF16)
        d = _dot(p, m_bf16)
        acc = d if acc is None else acc + d
        if i + 1 < terms:
            r = r - p.astype(F32)
    return acc


def _sigmoid(x):
    return 1.0 / (1.0 + jnp.exp(-x))


def _log_sigmoid_pair(z):
    ls = jnp.minimum(z, 0.0) - jnp.log(1.0 + jnp.exp(-jnp.abs(z)))
    return ls, ls - z


def _rms(x, g):
    return x * lax.rsqrt(jnp.mean(x * x, axis=-1, keepdims=True) + EPS) * g


def _ada_kernel(c_ref, w_ref, b_ref, o_ref):
    c = c_ref[...]
    s = (c * _sigmoid(c)).astype(BF16)
    o_ref[0] = _dot(s, w_ref[0].astype(BF16)) + b_ref[0]


def _ada(c_all, ada_w, ada_b):
    nb = c_all.shape[0]
    tn = 1536
    return pl.pallas_call(
        _ada_kernel,
        grid=(DEPTH, 6 * D_MODEL // tn),
        in_specs=[
            pl.BlockSpec((nb, D_MODEL), lambda l, j: (0, 0)),
            pl.BlockSpec((1, D_MODEL, tn), lambda l, j: (l, 0, j)),
            pl.BlockSpec((1, 1, tn), lambda l, j: (l, 0, j)),
        ],
        out_specs=pl.BlockSpec((1, nb, tn), lambda l, j: (l, 0, j)),
        out_shape=jax.ShapeDtypeStruct((DEPTH, nb, 6 * D_MODEL), F32),
        compiler_params=_cparams("arbitrary", "arbitrary"),
        name="ada",
    )(c_all, ada_w, ada_b.reshape(DEPTH, 1, 6 * D_MODEL))


def _token_tile(bx, tx):
    if tx >= ROW_TILE:
        return 1, ROW_TILE
    assert tx % SUBLANES == 0
    return min(bx, ROW_TILE // tx), tx


def _mod_spec(bb, chunk):
    return pl.BlockSpec((bb, 1, D_MODEL), lambda b, t: (b, 0, chunk))


def _inproj_kernel(x_ref, sh_ref, sc_ref, g_ref, w_ref, *rest, plan):
    n_prev = sum(kind == "cols_after" for _, _, kind in plan)
    prev_refs, o_refs = list(rest[:n_prev]), rest[n_prev:]
    bb, tt, _ = x_ref.shape
    h = _rms(x_ref[...], g_ref[0:1, :]) * (1.0 + sc_ref[...]) + sh_ref[...]
    y = _dot(h.reshape(bb * tt, D_MODEL).astype(BF16), w_ref[...])
    transposed = {}
    for o_ref, (c0, n, kind) in zip(o_refs, plan):
        blk = y[:, c0:c0 + n]
        if kind == "rows":
            o_ref[...] = blk.reshape(bb, tt, n)
        elif kind == "rows_q":
            o_ref[...] = (blk * SB_SCALE).astype(BF16).reshape(bb, tt, n)
        else:
            if c0 not in transposed:
                transposed[c0] = blk.T
            if kind == "cols_after":
                o_ref[0, 0] = prev_refs.pop(0)[0]
                o_ref[1, 0] = transposed[c0]
            else:
                o_ref[0] = transposed[c0].astype(o_ref.dtype)


def _in_proj(x, mod, norm_g, layer, w_bf16, w_layer, plan, prev=()):
    bx, tx, _ = x.shape
    bb, tm = _token_tile(bx, tx)
    n = w_bf16.shape[2]
    out_specs, out_shape, prev_specs = [], [], []
    for _, wd, kind in plan:
        assert bb == 1 or kind == "rows"
        if kind in ("rows", "rows_q"):
            out_specs.append(pl.BlockSpec((bb, tm, wd), lambda b, t: (b, t, 0)))
            out_shape.append(jax.ShapeDtypeStruct((bx, tx, wd), F32 if kind == "rows" else BF16))
        elif kind == "cols_after":
            prev_specs.append(pl.BlockSpec((1, wd, tm), lambda b, t: (b, 0, t)))
            out_specs.append(pl.BlockSpec((2, 1, wd, tm), lambda b, t: (0, b, 0, t)))
            out_shape.append(jax.ShapeDtypeStruct((2, bx, wd, tx), F32))
        else:
            out_specs.append(pl.BlockSpec((1, wd, tm), lambda b, t: (b, 0, t)))
            out_shape.append(jax.ShapeDtypeStruct((bx, wd, tx), F32 if kind == "cols" else BF16))
    assert len(prev_specs) == len(prev)
    return pl.pallas_call(
        functools.partial(_inproj_kernel, plan=tuple(plan)),
        grid=(bx // bb, tx // tm),
        in_specs=[
            pl.BlockSpec((bb, tm, D_MODEL), lambda b, t: (b, t, 0)),
            _mod_spec(bb, 0),
            _mod_spec(bb, 1),
            pl.BlockSpec((None, 4, D_MODEL), lambda b, t: (layer, 0, 0)),
            pl.BlockSpec((None, D_MODEL, n), lambda b, t: (w_layer, 0, 0)),
        ] + prev_specs,
        out_specs=out_specs,
        out_shape=out_shape,
        compiler_params=_cparams("arbitrary", "arbitrary"),
        name="in_proj",
    )(x, mod, mod, norm_g, w_bf16, *prev)


def _hgrn_direct(s, h, q_ref, i_ref, b_scr, kk_scr, oi_scr, o_scr, *, chunk):
    sub = min(chunk, HG_DIRECT_SUB)
    sl = slice(h * HG_D, (h + 1) * HG_D)
    rows = lax.broadcasted_iota(jnp.int32, (chunk, 1), 0)
    srow = lax.broadcasted_iota(jnp.int32, (sub, 1), 0)

    def sub_block(i, carry):
        r0 = pl.multiple_of(i * sub, sub)
        b_i = b_scr[s, pl.ds(r0, sub), sl]
        q_i = q_ref[s, pl.ds(r0, sub), sl]
        kk_i = kk_scr[s, pl.ds(r0, sub), sl]
        v_i = i_ref[s, pl.ds(r0, sub), sl]
        o = oi_scr[s, pl.ds(r0, sub), sl]
        if chunk > sub:
            anchor = b_i[0:1, :]
            qt = q_i * jnp.exp(b_i - anchor)
            kt = jnp.where(rows < r0, kk_scr[s, :, sl] * jnp.exp(jnp.minimum(anchor - b_scr[s, :, sl], 0.0)), 0.0)
            a_off = _dot_nt(qt.astype(BF16), kt.astype(BF16))
            o = o + _dot(a_off.astype(BF16), i_ref[s, :, sl].astype(BF16))
        o_diag = jnp.zeros((sub, HG_D), F32)
        for tt in range(sub):
            e = jnp.exp(jnp.minimum(b_i[tt:tt + 1, :] - b_i, 0.0))
            d = jnp.sum(q_i[tt:tt + 1, :] * e * kk_i, axis=-1, keepdims=True)
            d = jnp.where(srow <= tt, d, 0.0)
            row = jnp.sum(d * v_i, axis=0, keepdims=True)
            o_diag = jnp.where(srow == tt, row, o_diag)
        o_scr[s, pl.ds(r0, sub), sl] = o + o_diag
        return carry

    lax.fori_loop(0, chunk // sub, sub_block, 0)


def _hgrn_kernel(q_ref, f_ref, i_ref, g_ref, lb_ref, gn_ref, s0_ref, *rest,
                 nb, chunk, sub, single, stacked, safe_decay):
    if stacked:
        sprev_ref, o_ref, sout2_ref, st_scr, b_scr, kk_scr, oi_scr, o_scr = rest
        sout2_ref[0] = sprev_ref[...]
        sout_ref = sout2_ref.at[1]
    else:
        o_ref, sout_ref, st_scr, b_scr, kk_scr, oi_scr, o_scr = rest
    t = pl.program_id(1)
    if not single:
        @pl.when(t == 0)
        def _():
            for s in range(nb):
                for h in range(HG_HEADS):
                    st_scr[s * HG_HEADS + h] = s0_ref[s, h].T

    worst = [_hgrn_main(s, q_ref, f_ref, i_ref, lb_ref, s0_ref, sout_ref, st_scr, b_scr, kk_scr, oi_scr, o_scr,
                        chunk=chunk, sub=sub, single=single, safe_decay=safe_decay) for s in range(nb)]

    for s in range(nb):
        @pl.when(jnp.max(worst[s]) > safe_decay)
        def _(s=s):
            for h in range(HG_HEADS):
                _hgrn_direct(s, h, q_ref, i_ref, b_scr, kk_scr, oi_scr, o_scr, chunk=chunk)

    gn = gn_ref[...]
    for s in range(nb):
        g = g_ref[s]
        for h in range(HG_HEADS):
            sl = slice(h * HG_D, (h + 1) * HG_D)
            gh = g[:, sl]
            o_ref[s, :, sl] = _rms(o_scr[s, :, sl], gn[:, sl]) * (gh * _sigmoid(gh))

    if not single:
        @pl.when(t == pl.num_programs(1) - 1)
        def _():
            for s in range(nb):
                for h in range(HG_HEADS):
                    sout_ref[s, h] = st_scr[s * HG_HEADS + h].T


def _hgrn_main(s, q_ref, f_ref, i_ref, lb_ref, s0_ref, sout_ref, st_scr, b_scr, kk_scr, oi_scr, o_scr,
               *, chunk, sub, single, safe_decay):
    lb = lb_ref[...]
    q = q_ref[s]
    fl = f_ref[s]
    v = i_ref[s]
    e = jnp.exp(-jnp.abs(fl))
    d = 1.0 + e
    r = 1.0 / d
    c = jnp.log1p(-lb) + (jnp.minimum(fl, 0.0) - jnp.log(d))
    a = jnp.log(lb)
    log_f = jnp.maximum(a, c) + jnp.log(1.0 + jnp.exp(-jnp.abs(a - c)))
    kk = (1.0 - lb) * jnp.where(fl >= 0.0, e * r, r)
    ri = lax.broadcasted_iota(jnp.int32, (chunk, chunk), 0)
    ci = lax.broadcasted_iota(jnp.int32, (chunk, chunk), 1)
    tril = jnp.where(ri >= ci, 1.0, 0.0).astype(BF16)
    p1 = log_f.astype(BF16)
    r1 = log_f - p1.astype(F32)
    p2 = r1.astype(BF16)
    p3 = (r1 - p2.astype(F32)).astype(BF16)
    b3 = _dot(tril, jnp.concatenate([p1, p2, p3], axis=1))
    b = b3[:, 0:MIX_HALF] + b3[:, MIX_HALF:2 * MIX_HALF] + b3[:, 2 * MIX_HALF:3 * MIX_HALF]
    b_scr[s] = b
    kk_scr[s] = kk
    b_last = b[chunk - 1:chunk, :]
    qe = (q * jnp.exp(b)).astype(BF16)
    kd = kk * jnp.exp(b_last - b)
    dec = jnp.exp(b_last)
    npad = HG_D - chunk
    prow = lax.broadcasted_iota(jnp.int32, (HG_D, 1), 0)
    for h in range(HG_HEADS):
        sl = slice(h * HG_D, (h + 1) * HG_D)
        if single:
            st = s0_ref[s, h]
            oi_scr[s, :, sl] = _dot(qe[:, sl], st.astype(BF16))
            zpad = jnp.zeros((npad, HG_D), F32)
            m = jnp.concatenate([kd[:, sl], zpad], axis=0)
            m = jnp.where(prow == chunk, dec[:, sl], m)
            mt = m.T
            vpad = jnp.concatenate([v[:, sl], zpad], axis=0)
            sout_ref[s, h] = mt[:, chunk:chunk + 1] * st + _dot(mt.astype(BF16), vpad.astype(BF16))
        else:
            st = st_scr[s * HG_HEADS + h]
            oi_scr[s, :, sl] = _dot_nt(qe[:, sl], st.astype(BF16))
            st_scr[s * HG_HEADS + h] = dec[:, sl] * st + _dot(v[:, sl].T.astype(BF16), kd[:, sl].astype(BF16))

    n_sub = chunk // sub
    worst = jnp.zeros((1, MIX_HALF), F32)
    if single:
        zpad = jnp.zeros((npad, MIX_HALF), F32)
        kk_full = jnp.concatenate([kk, zpad], axis=0)
        b_full = jnp.concatenate([b, zpad], axis=0)
        v_full = jnp.concatenate([v, zpad], axis=0)
    else:
        kk_full, b_full, v_full = kk, b, v
    anchors = [b[i * sub:i * sub + 1, :] for i in range(n_sub)]
    for i in range(n_sub):
        worst = jnp.maximum(worst, anchors[i] - b[(i + 1) * sub - 1:(i + 1) * sub, :])
    anchor_rows = jnp.concatenate([jnp.broadcast_to(an, (sub, MIX_HALF)) for an in anchors], axis=0)
    qt = (q * jnp.exp(b - anchor_rows)).astype(BF16)
    kts = []
    for i, an in enumerate(anchors):
        n = kk_full.shape[0] if single else (i + 1) * sub
        kt = (kk_full[0:n] * jnp.exp(jnp.minimum(an - b_full[0:n], safe_decay))).astype(BF16)
        if n < kk_full.shape[0]:
            kt = jnp.concatenate([kt, jnp.zeros((kk_full.shape[0] - n, MIX_HALF), BF16)], axis=0)
        kts.append(kt)
    row_blk = lax.broadcasted_iota(jnp.int32, (chunk, HG_D), 0) // sub
    causal = (lax.broadcasted_iota(jnp.int32, (chunk, HG_D), 1)
              <= lax.broadcasted_iota(jnp.int32, (chunk, HG_D), 0))
    zero_q = jnp.zeros((chunk, HG_D), BF16)
    atts = []
    for h in range(HG_HEADS):
        sl = slice(h * HG_D, (h + 1) * HG_D)
        q_cat = jnp.concatenate([jnp.where(row_blk == i, qt[:, sl], zero_q) for i in range(n_sub)], axis=1)
        k_cat = jnp.concatenate([kt[:, sl] for kt in kts], axis=1)
        atts.append(jnp.where(causal, _dot_nt(q_cat, k_cat), 0.0).astype(BF16))
    bd = (lax.broadcasted_iota(jnp.int32, (HG_HEADS * HG_D, MIX_HALF), 0) // HG_D
          == lax.broadcasted_iota(jnp.int32, (HG_HEADS * HG_D, MIX_HALF), 1) // HG_D)
    v_bd = jnp.where(bd, jnp.concatenate([v_full] * HG_HEADS, axis=0), 0.0).astype(BF16)
    o_scr[s] = oi_scr[s] + _dot(jnp.concatenate(atts, axis=1), v_bd)
    return worst


def _hgrn(qfig, lb_l, gn_l, s0, s0_layer=0, s_prev=None, safe_decay=HG_SAFE_DECAY):
    bx, tx, _ = qfig.shape
    chunk = min(tx, HG_CHUNK)
    sub = min(chunk, HG_SUB)
    single = tx == chunk
    assert not single or chunk < HG_D
    nb = HG_SEQS if bx % HG_SEQS == 0 else 1

    def col(k):
        return pl.BlockSpec((nb, chunk, MIX_HALF), lambda b, t: (b, t, k))

    vec = pl.BlockSpec((1, MIX_HALF), lambda b, t: (0, 0))
    state = pl.BlockSpec((nb, HG_HEADS, HG_D, HG_D), lambda b, t: (b, 0, 0, 0))
    state_in = pl.BlockSpec((None, nb, HG_HEADS, HG_D, HG_D), lambda b, t: (s0_layer, b, 0, 0, 0))
    stacked = s_prev is not None
    if stacked:
        state_out = pl.BlockSpec((2, nb, HG_HEADS, HG_D, HG_D), lambda b, t: (0, b, 0, 0, 0))
        state_shape = jax.ShapeDtypeStruct((2, bx, HG_HEADS, HG_D, HG_D), F32)
    else:
        state_out = state
        state_shape = jax.ShapeDtypeStruct((bx, HG_HEADS, HG_D, HG_D), F32)
    return pl.pallas_call(
        functools.partial(_hgrn_kernel, nb=nb, chunk=chunk, sub=sub, single=single, stacked=stacked,
                          safe_decay=safe_decay),
        grid=(bx // nb, tx // chunk),
        in_specs=[col(0), col(1), col(2), col(3), vec, vec, state_in] + ([state] if stacked else []),
        out_specs=[pl.BlockSpec((nb, chunk, MIX_HALF), lambda b, t: (b, t, 0)), state_out],
        out_shape=[jax.ShapeDtypeStruct((bx, tx, MIX_HALF), F32), state_shape],
        scratch_shapes=([pltpu.VMEM((nb * HG_HEADS, HG_D, HG_D), F32)]
                        + [pltpu.VMEM((nb, chunk, MIX_HALF), F32)] * 4),
        compiler_params=_cparams("arbitrary", "arbitrary"),
        name="hgrn",
    )(qfig, qfig, qfig, qfig, lb_l, gn_l, s0, *([s_prev] if stacked else []))


def _sb_prompt_kernel(bias_ref, q_ref, kt_ref, vt_ref, o_ref, acc_scr, carry_scr, *, blk, per_trip):
    g = pl.program_id(1)
    qi = pl.program_id(2)
    lane = lax.broadcasted_iota(jnp.int32, (1, LANES), 1)
    ri = lax.broadcasted_iota(jnp.int32, (blk, blk), 0)
    ci = lax.broadcasted_iota(jnp.int32, (blk, blk), 1)
    upper = jnp.where(ri > ci, 1.0, 0.0).astype(BF16)
    ones = jnp.ones((blk, LANES), BF16)
    tri = ci < ri
    causal = jnp.concatenate([tri, tri], axis=0)
    second = lax.broadcasted_iota(jnp.int32, (2 * blk, 1), 0) >= blk
    qs, bias = [], []
    for p in range(SB_PAIRS):
        q = q_ref[0, :, p * LANES:(p + 1) * LANES]
        zero = jnp.zeros_like(q)
        qs.append(jnp.concatenate([jnp.where(lane < SB_HD, q, zero), jnp.where(lane >= SB_HD, q, zero)], axis=0))
        h0 = 2 * (g * SB_PAIRS + p)
        bias.append(jnp.where(second, bias_ref[h0 + 1], bias_ref[h0]))
    acc_scr[...] = jnp.zeros_like(acc_scr)
    carry_scr[...] = jnp.zeros_like(carry_scr)

    def step(kbs, masked):
        for p in range(SB_PAIRS):
            carry = carry_scr[p]
            pv = None
            for kb in kbs:
                c0 = pl.multiple_of(kb * blk, blk)
                kblk = kt_ref[0, p * LANES:(p + 1) * LANES, pl.ds(c0, blk)]
                vblk = vt_ref[0, p * LANES:(p + 1) * LANES, pl.ds(c0, blk)]
                ls, lk = _log_sigmoid_pair(_dot(qs[p], kblk) + bias[p])
                if masked:
                    lk = jnp.where(causal, lk, 0.0)
                lkb = lk.astype(BF16)
                w = jnp.exp(ls + _dot(lkb, upper) + jnp.concatenate([carry] * (blk // LANES), axis=1))
                if masked:
                    w = jnp.where(causal, w, 0.0)
                pv_k = _dot_nt(w.astype(BF16), vblk)
                pv = pv_k if pv is None else pv + pv_k
                carry = carry + _dot(lkb, ones)
            acc_scr[p] += pv
            carry_scr[p] = carry

    step([qi], True)

    def body(j, c):
        kb = qi - 1 - per_trip * j
        step([kb - i for i in range(per_trip)], False)
        return c

    lax.fori_loop(0, qi // per_trip, body, 0)
    left = qi % per_trip

    def tail(j, c):
        step([left - 1 - j], False)
        return c

    lax.fori_loop(0, left, tail, 0)
    for p in range(SB_PAIRS):
        o_ref[0, :, p * LANES:(p + 1) * LANES] = jnp.where(lane < SB_HD, acc_scr[p, 0:blk], acc_scr[p, blk:2 * blk])


def _sb_prompt(bias_l, q_bf16, kt_bf16, vt_bf16):
    bx, tx, _ = q_bf16.shape
    blk = SB_BLOCK
    wd = SB_PAIRS * LANES
    rows = pl.BlockSpec((1, blk, wd), lambda b, g, qi: (b, qi, g))
    full = pl.BlockSpec((1, wd, tx), lambda b, g, qi: (b, g, 0))
    return pl.pallas_call(
        functools.partial(_sb_prompt_kernel, blk=blk, per_trip=SB_PER_TRIP),
        grid=(bx, MIX_HALF // wd, tx // blk),
        in_specs=[pl.BlockSpec(memory_space=pltpu.SMEM), rows, full, full],
        out_specs=rows,
        out_shape=jax.ShapeDtypeStruct((bx, tx, MIX_HALF), F32),
        scratch_shapes=[pltpu.VMEM((SB_PAIRS, 2 * blk, LANES), F32), pltpu.VMEM((SB_PAIRS, 2 * blk, LANES), F32)],
        compiler_params=_cparams("arbitrary", "arbitrary", "arbitrary"),
        name="sb_prompt",
    )(bias_l, q_bf16, kt_bf16, vt_bf16)


def _sb_sample_kernel(pt_ref, q_ref, kn_ref, vn_ref, bias_ref, *rest, tq, page, n_pages):
    kc_refs, vc_refs, o_ref = rest[:n_pages], rest[n_pages:2 * n_pages], rest[2 * n_pages]
    nrow = SB_HEADS * tq
    rowh = lax.broadcasted_iota(jnp.int32, (nrow, MIX_HALF), 0) // tq
    colh = lax.broadcasted_iota(jnp.int32, (nrow, MIX_HALF), 1) // SB_HD
    own = rowh == colh
    bias = bias_ref[...]
    ri = lax.broadcasted_iota(jnp.int32, (page, page), 0)
    ci = lax.broadcasted_iota(jnp.int32, (page, page), 1)
    upper = jnp.where(ri > ci, 1.0, 0.0).astype(BF16)
    qbd = (jnp.where(own, jnp.concatenate([q_ref[0]] * SB_HEADS, axis=0), 0.0) * SB_SCALE).astype(BF16)
    pad = jnp.zeros((page - tq, MIX_HALF), F32)
    kn = jnp.concatenate([kn_ref[0], pad], axis=0).astype(BF16)
    vn = jnp.concatenate([vn_ref[0], pad], axis=0).astype(BF16)
    qpos = lax.broadcasted_iota(jnp.int32, (nrow, page), 0) % tq
    kpos = lax.broadcasted_iota(jnp.int32, (nrow, page), 1)
    new_mask = kpos < qpos

    ls0, lk0 = _log_sigmoid_pair(_dot_nt(qbd, kn) + bias)
    ls_all, lk_all = [ls0], [jnp.where(new_mask, lk0, 0.0)]
    for j in range(n_pages):
        ls, lk = _log_sigmoid_pair(_dot(qbd, kc_refs[j][0, 0].astype(BF16)) + bias)
        ls_all.append(ls)
        lk_all.append(lk)
    lk_cat = jnp.concatenate(lk_all, axis=0)
    suffix = _split_dot(lk_cat, upper, 2)
    total = jnp.sum(lk_cat, axis=-1, keepdims=True)
    carry = jnp.zeros((nrow, 1), F32)
    acc = jnp.zeros((nrow, MIX_HALF), F32)
    for j in range(n_pages + 1):
        r = slice(j * nrow, (j + 1) * nrow)
        w = jnp.exp(ls_all[j] + suffix[r] + carry)
        if j == 0:
            acc = acc + _dot(jnp.where(new_mask, w, 0.0).astype(BF16), vn)
        else:
            acc = acc + _dot_nt(w.astype(BF16), vc_refs[j - 1][0, 0].astype(BF16))
        carry = carry + total[r]
    masked = jnp.where(own, acc, 0.0)
    out = masked[0:tq, :]
    for h in range(1, SB_HEADS):
        out = out + masked[h * tq:(h + 1) * tq, :]
    o_ref[0] = out


def _sb_sample(layer, page_table, bias_rows, q, k_new, v_new, cache_k, cache_v):
    bx, tq, _ = q.shape
    n_pages = page_table.shape[1]
    page = cache_k.shape[3]
    nrow = SB_HEADS * tq
    seq = pl.BlockSpec((1, tq, MIX_HALF), lambda b, pt: (b, 0, 0))

    def cache(j):
        return pl.BlockSpec((1, 1, MIX_HALF, page),
                            lambda b, pt: (layer, pt[b * n_pages + (n_pages - 1 - j)], 0, 0))

    pages = [cache(j) for j in range(n_pages)]
    return pl.pallas_call(
        functools.partial(_sb_sample_kernel, tq=tq, page=page, n_pages=n_pages),
        grid_spec=pltpu.PrefetchScalarGridSpec(
            num_scalar_prefetch=1,
            grid=(bx,),
            in_specs=[seq, seq, seq, pl.BlockSpec((nrow, 1), lambda b, pt: (0, 0))] + pages + pages,
            out_specs=seq,
        ),
        out_shape=jax.ShapeDtypeStruct((bx, tq, MIX_HALF), F32),
        compiler_params=_cparams("arbitrary"),
        name="sb_sample",
    )(page_table.reshape(-1), q, k_new, v_new, bias_rows, *([cache_k] * n_pages), *([cache_v] * n_pages))


def _odd_kernel(*refs, nb, tm, q_start):
    for s in range(nb):
        _odd_one(s, *refs, tm=tm, q_start=q_start)


def _odd_one(s, u_ref, a_ref, gt_ref, hu_ref, hg_ref, pw_ref, ps_ref, cw_ref, cb_ref, lg_ref, lbias_ref,
             po_ref, co_ref, nu_ref, ng_ref, eu_scr, eg_scr, sh_scr, *, tm, q_start):
    t = pl.program_id(1)
    hu0 = 2 * SUBLANES
    hg0 = 4 * SUBLANES

    @pl.when(t == 0)
    def _():
        eu_scr[0:hu0 - POOL_HIST, :] = jnp.zeros((hu0 - POOL_HIST, MIX_HALF), F32)
        eg_scr[0:hg0 - CONV_HIST, :] = jnp.zeros((hg0 - CONV_HIST, MIX_HALF), F32)
        eu_scr[hu0 - POOL_HIST:hu0, :] = hu_ref[s]
        eg_scr[hg0 - CONV_HIST:hg0, :] = hg_ref[s]

    @pl.when(t > 0)
    def _():
        eu_scr[0:hu0, :] = eu_scr[tm:tm + hu0, :]
        eg_scr[0:hg0, :] = eg_scr[tm:tm + hg0, :]

    u = u_ref[s]
    gt = gt_ref[s]
    glu = a_ref[s] * _sigmoid(gt)
    eu_scr[hu0:hu0 + tm, :] = u
    eg_scr[hg0:hg0 + tm, :] = glu

    pos = q_start + t * tm + lax.broadcasted_iota(jnp.int32, (tm, 1), 0)
    for gi, w in enumerate(POOL_WINDOWS):
        c0 = gi * POOL_GC
        wsum = u[:, c0:c0 + POOL_GC]
        for d in range(1, w):
            wsum = wsum + eu_scr[hu0 - d:hu0 - d + tm, c0:c0 + POOL_GC]
        cnt = jnp.minimum(w, pos + 1).astype(F32)
        pooled = wsum / cnt - u[:, c0:c0 + POOL_GC]
        po = _dot(pooled.astype(BF16), pw_ref[gi].astype(BF16))
        po_ref[s, :, c0:c0 + POOL_GC] = po * ps_ref[:, c0:c0 + POOL_GC]

    cv = jnp.broadcast_to(cb_ref[...], (tm, MIX_HALF))
    first = hg0 - CONV_HIST
    for r in range(SUBLANES):
        taps = [j for j in range(CONV_WIDTH) if (first + j) % SUBLANES == r]
        if not taps:
            continue
        span = (first + taps[-1]) // SUBLANES * SUBLANES
        if r == 0:
            src = eg_scr
        else:
            sh_scr[0:span + tm, :] = eg_scr[r:r + span + tm, :]
            src = sh_scr
        for j in taps:
            a0 = first + j - r
            cv = cv + cw_ref[j:j + 1, :] * src[a0:a0 + tm, :]
    mu = jnp.mean(cv, axis=-1, keepdims=True)
    dv = cv - mu
    var = jnp.mean(dv * dv, axis=-1, keepdims=True)
    y = dv * lax.rsqrt(var + EPS) * lg_ref[...] + lbias_ref[...]
    co_ref[s] = y * _sigmoid(y)

    @pl.when(t == pl.num_programs(1) - 1)
    def _():
        nu_ref[s] = eu_scr[hu0 + tm - POOL_HIST:hu0 + tm, :]
        ng_ref[s] = eg_scr[hg0 + tm - CONV_HIST:hg0 + tm, :]


def _odd(u, a, gt, hist_u, hist_g, io, pool_w, pool_scale, conv_w, conv_b, ln_g, ln_b, q_start):
    bx, tx, _ = u.shape
    tm = min(tx, 256)
    nb = ODD_SEQS if (tx == tm and bx % ODD_SEQS == 0) else 1
    tile = pl.BlockSpec((nb, tm, MIX_HALF), lambda b, t: (b, t, 0))
    vec = pl.BlockSpec((1, MIX_HALF), lambda b, t: (0, 0))
    hu_spec = pl.BlockSpec((None, nb, POOL_HIST, MIX_HALF), lambda b, t: (io if hist_u.shape[0] > 1 else 0, b, 0, 0))
    hg_spec = pl.BlockSpec((None, nb, CONV_HIST, MIX_HALF), lambda b, t: (io if hist_g.shape[0] > 1 else 0, b, 0, 0))
    return pl.pallas_call(
        functools.partial(_odd_kernel, nb=nb, tm=tm, q_start=q_start),
        grid=(bx // nb, tx // tm),
        in_specs=[tile, tile, tile, hu_spec, hg_spec,
                  pl.BlockSpec((None, len(POOL_WINDOWS), POOL_GC, POOL_GC), lambda b, t: (io, 0, 0, 0)),
                  vec,
                  pl.BlockSpec((None, CONV_WIDTH, MIX_HALF), lambda b, t: (io, 0, 0)),
                  vec, vec, vec],
        out_specs=[tile, tile,
                   pl.BlockSpec((nb, POOL_HIST, MIX_HALF), lambda b, t: (b, 0, 0)),
                   pl.BlockSpec((nb, CONV_HIST, MIX_HALF), lambda b, t: (b, 0, 0))],
        out_shape=[jax.ShapeDtypeStruct((bx, tx, MIX_HALF), F32), jax.ShapeDtypeStruct((bx, tx, MIX_HALF), F32),
                   jax.ShapeDtypeStruct((bx, POOL_HIST, MIX_HALF), F32),
                   jax.ShapeDtypeStruct((bx, CONV_HIST, MIX_HALF), F32)],
        scratch_shapes=[pltpu.VMEM((2 * SUBLANES + tm, MIX_HALF), F32),
                        pltpu.VMEM((4 * SUBLANES + tm, MIX_HALF), F32),
                        pltpu.VMEM((4 * SUBLANES + tm, MIX_HALF), F32)],
        compiler_params=_cparams("arbitrary", "arbitrary"),
        name="odd_mixer",
    )(u, a, gt, hist_u, hist_g, pool_w, pool_scale[io:io + 1], conv_w, conv_b[io:io + 1], ln_g[io:io + 1],
      ln_b[io:io + 1])


def _post_kernel(a_ref, b_ref, x_ref, g1_ref, sh2_ref, sc2_ref, g2_ref, ng_ref, woa_ref, wob_ref,
                 w1_ref, w2_ref, o_ref):
    bb, tt, _ = x_ref.shape
    rows = bb * tt
    m = (_dot(a_ref[...].reshape(rows, MIX_HALF).astype(BF16), woa_ref[...])
         + _dot(b_ref[...].reshape(rows, MIX_HALF).astype(BF16), wob_ref[...]))
    x1 = x_ref[...] + g1_ref[...] * _rms(m, ng_ref[1:2, :]).reshape(bb, tt, D_MODEL)
    h2 = (_rms(x1, ng_ref[2:3, :]) * (1.0 + sc2_ref[...]) + sh2_ref[...]).reshape(rows, D_MODEL).astype(BF16)
    acc = None
    for c0 in range(0, D_FF, FF_TILE):
        hc = jnp.maximum(_dot(h2, w1_ref[:, c0:c0 + FF_TILE]), 0.0)
        d = _dot((hc * hc).astype(BF16), w2_ref[c0:c0 + FF_TILE, :])
        acc = d if acc is None else acc + d
    o_ref[...] = x1 + g2_ref[...] * _rms(acc, ng_ref[3:4, :]).reshape(bb, tt, D_MODEL)


def _post(a, b, x, mod, norm_g, layer, wo_bf16, wo_layer, w1_bf16, w2_bf16):
    bx, tx, _ = x.shape
    bb, tm = _token_tile(bx, tx)
    half = pl.BlockSpec((bb, tm, MIX_HALF), lambda b_, t: (b_, t, 0))
    full = pl.BlockSpec((bb, tm, D_MODEL), lambda b_, t: (b_, t, 0))
    once = pl.Buffered(1)
    return pl.pallas_call(
        _post_kernel,
        grid=(bx // bb, tx // tm),
        in_specs=[half, half, full,
                  _mod_spec(bb, 2), _mod_spec(bb, 3), _mod_spec(bb, 4), _mod_spec(bb, 5),
                  pl.BlockSpec((None, 4, D_MODEL), lambda b_, t: (layer, 0, 0)),
                  pl.BlockSpec((None, MIX_HALF, D_MODEL), lambda b_, t: (wo_layer, 0, 0), pipeline_mode=once),
                  pl.BlockSpec((None, MIX_HALF, D_MODEL), lambda b_, t: (wo_layer, 1, 0), pipeline_mode=once),
                  pl.BlockSpec((None, D_MODEL, D_FF), lambda b_, t: (layer, 0, 0), pipeline_mode=once),
                  pl.BlockSpec((None, D_FF, D_MODEL), lambda b_, t: (layer, 0, 0), pipeline_mode=once)],
        out_specs=full,
        out_shape=jax.ShapeDtypeStruct((bx, tx, D_MODEL), F32),
        compiler_params=_cparams("arbitrary", "arbitrary"),
        name="post",
    )(a, b, x, mod, mod, mod, mod, norm_g, wo_bf16, wo_bf16, w1_bf16, w2_bf16)


def kernel(x_prompt, x_sample, c_prompt, c_sample, cache_sb_k, cache_sb_v, state_hgrn, state_pool, state_conv,
           page_table, norm_g, ada_w, ada_b, even_w_in, even_w_out, sb_bias, hgrn_lb_logits, hgrn_norm_g,
           odd_w_in, odd_w_out, pool_w, pool_scale, conv_w, conv_b, conv_ln_g, conv_ln_b, mlp_w1, mlp_w2):
    n_even = even_w_in.shape[0]
    assert n_even == 2 and DEPTH == 4
    bp, tp, _ = x_prompt.shape
    bs, ts, _ = x_sample.shape
    n_pool, page = cache_sb_k.shape[1], cache_sb_k.shape[2]
    past_len = page_table.shape[1] * page
    half = MIX_HALF

    lb_cum = jnp.cumsum(jax.nn.softmax(hgrn_lb_logits.astype(F32), axis=0), axis=0)
    lower_bounds = jnp.maximum(lb_cum - lb_cum[:1], 0.0)

    mod_all = _ada(jnp.concatenate([c_prompt, c_sample], axis=0), ada_w, ada_b)
    even_w_in_b = even_w_in.astype(BF16)
    even_w_out_b = even_w_out.astype(BF16)
    odd_w_in_b = odd_w_in.astype(BF16)
    odd_w_out_b = odd_w_out.astype(BF16)
    w1_b = mlp_w1.astype(BF16)
    w2_b = mlp_w2.astype(BF16)
    cache_k = cache_sb_k.transpose(0, 1, 3, 4, 2).reshape(n_even, n_pool, half, page)
    cache_v = cache_sb_v.transpose(0, 1, 3, 4, 2).reshape(n_even, n_pool, half, page)
    gn = hgrn_norm_g.reshape(n_even, half)

    def run(x, mods, is_sample):
        bx, tx, _ = x.shape
        s_list, k_list, v_list, p_list, c_list = [], [], [], [], []
        for l in range(DEPTH):
            mod = mods[l]
            if l % 2 == 0:
                ie = l // 2
                if is_sample:
                    qfig, q_b, k_b, v_b = _in_proj(
                        x, mod, norm_g, l, even_w_in_b, ie,
                        [(0, 4 * half, "rows"), (4 * half, half, "rows"), (5 * half, half, "rows"),
                         (6 * half, half, "rows")])
                    o_a, s_new = _hgrn(qfig.reshape(bs, ts, 4 * half), lower_bounds[ie:ie + 1], gn[ie:ie + 1],
                                       state_hgrn, ie, s_prev=s_list[0] if ie == 1 else None)
                    bias_rows = jnp.repeat(sb_bias[ie].astype(F32), ts).reshape(SB_HEADS * ts, 1)
                    k_b = k_b.reshape(bs, ts, half)
                    v_b = v_b.reshape(bs, ts, half)
                    o_b = _sb_sample(ie, page_table, bias_rows, q_b.reshape(bs, ts, half), k_b, v_b, cache_k, cache_v)
                    k_list.append(k_b.reshape(bs, ts, SB_HEADS, SB_HD))
                    v_list.append(v_b.reshape(bs, ts, SB_HEADS, SB_HD))
                else:
                    kv_kind = "cols" if ie == 0 else "cols_after"
                    qfig, q_b, k_t, k_tb, v_t, v_tb = _in_proj(
                        x, mod, norm_g, l, even_w_in_b, ie,
                        [(0, 4 * half, "rows"), (4 * half, half, "rows_q"), (5 * half, half, kv_kind),
                         (5 * half, half, "cols_bf16"), (6 * half, half, kv_kind), (6 * half, half, "cols_bf16")],
                        prev=() if ie == 0 else (k_list[0], v_list[0]))
                    o_a, s_new = _hgrn(qfig, lower_bounds[ie:ie + 1], gn[ie:ie + 1],
                                       jnp.zeros((1, bx, HG_HEADS, HG_D, HG_D), F32))
                    o_b = _sb_prompt(sb_bias[ie].astype(F32), q_b, k_tb, v_tb)
                    k_list.append(k_t)
                    v_list.append(v_t)
                mix_a, mix_b, w_out, wo_layer = o_a.reshape(bx, tx, half), o_b.reshape(bx, tx, half), even_w_out_b, ie
                s_list.append(s_new)
            else:
                io = l // 2
                u, a, gt = _in_proj(x, mod, norm_g, l, odd_w_in_b, io,
                                    [(0, half, "rows"), (half, half, "rows"), (2 * half, half, "rows")])
                if is_sample:
                    hb, ht = bs, ts
                    hist_u, hist_g, q_start = state_pool, state_conv, past_len
                else:
                    hb, ht = bx, tx
                    hist_u = jnp.zeros((1, bx, POOL_HIST, half), F32)
                    hist_g = jnp.zeros((1, bx, CONV_HIST, half), F32)
                    q_start = 0
                po, co, new_u, new_g = _odd(u.reshape(hb, ht, half), a.reshape(hb, ht, half),
                                            gt.reshape(hb, ht, half), hist_u, hist_g, io, pool_w, pool_scale,
                                            conv_w, conv_b, conv_ln_g, conv_ln_b, q_start)
                mix_a, mix_b, w_out, wo_layer = po.reshape(bx, tx, half), co.reshape(bx, tx, half), odd_w_out_b, io
                p_list.append(new_u)
                c_list.append(new_g)
            x = _post(mix_a, mix_b, x, mod, norm_g, l, w_out, wo_layer, w1_b, w2_b)
        if is_sample:
            s_all, k_all, v_all = s_list[1], jnp.stack(k_list), jnp.stack(v_list)
        else:
            s_all = jnp.stack(s_list)
            k_all = k_list[1].reshape(2, bx, SB_HEADS, SB_HD, tx).transpose(0, 1, 4, 2, 3)
            v_all = v_list[1].reshape(2, bx, SB_HEADS, SB_HD, tx).transpose(0, 1, 4, 2, 3)
        return (x, s_all, k_all, v_all, jnp.stack(p_list), jnp.stack(c_list))

    mods_p = [mod_all[l, :bp].reshape(bp, 1, 6 * D_MODEL) for l in range(DEPTH)]
    mods_s = [mod_all[l, bp:].reshape(bs, 1, 6 * D_MODEL) for l in range(DEPTH)]
    y_p, s_p, k_p, v_p, pool_p, conv_p = run(x_prompt, mods_p, False)
    y_s, s_s, k_s, v_s, pool_s, conv_s = run(x_sample, mods_s, True)
    return (y_p, y_s, k_p, v_p, k_s, v_s, s_p, s_s, pool_p, pool_s, conv_p, conv_s)
```

```python
import functools

import jax
import jax.numpy as jnp
from jax import lax
from jax.experimental import pallas as pl
from jax.experimental.pallas import tpu as pltpu

F32 = jnp.float32
BF16 = jnp.bfloat16

D_MODEL = 1024
DEPTH = 4
MIX_HALF = D_MODEL // 2
HG_HEADS = 4
HG_D = MIX_HALF // HG_HEADS
SB_HEADS = 8
SB_HD = MIX_HALF // SB_HEADS
SB_SCALE = SB_HD ** -0.5
POOL_WINDOWS = (2, 4, 8, 16)
POOL_GC = MIX_HALF // len(POOL_WINDOWS)
POOL_HIST = max(POOL_WINDOWS) - 1
CONV_WIDTH = 31
CONV_HIST = CONV_WIDTH - 1
D_FF = 4 * D_MODEL
EPS = 1e-6

LANES = 128
SUBLANES = 8
VMEM_LIMIT = 52 * 1024 * 1024

ROW_TILE = 512
FF_TILE = 1024
HG_CHUNK = 128
HG_SUB = 16
HG_SAFE_DECAY = 72.0
HG_DIRECT_SUB = 16
HG_SEQS = 4
ODD_SEQS = 4
SB_BLOCK = 256
SB_PER_TRIP = 2
SB_PAIRS = 2


def _cparams(*sem):
    return pltpu.CompilerParams(dimension_semantics=sem, vmem_limit_bytes=VMEM_LIMIT)


def _dot(a, b):
    return jnp.dot(a, b, preferred_element_type=F32)


def _dot_nt(a, b):
    return lax.dot_general(a, b, (((1,), (1,)), ((), ())), preferred_element_type=F32)


def _split_dot(x, m_bf16, terms):
    acc = None
    r = x
    for i in range(terms):
        p = r.astype(BF16)
        d = _dot(p, m_bf16)
        acc = d if acc is None else acc + d
        if i + 1 < terms:
            r = r - p.astype(F32)
    return acc


def _sigmoid(x):
    return 1.0 / (1.0 + jnp.exp(-x))


def _log_sigmoid_pair(z):
    ls = jnp.minimum(z, 0.0) - jnp.log(1.0 + jnp.exp(-jnp.abs(z)))
    return ls, ls - z


def _rms(x, g):
    return x * lax.rsqrt(jnp.mean(x * x, axis=-1, keepdims=True) + EPS) * g


def _ada_kernel(c_ref, w_ref, b_ref, o_ref):
    c = c_ref[...]
    s = (c * _sigmoid(c)).astype(BF16)
    o_ref[0] = _dot(s, w_ref[0].astype(BF16)) + b_ref[0]


def _ada(c_all, ada_w, ada_b):
    nb = c_all.shape[0]
    tn = 1536
    return pl.pallas_call(
        _ada_kernel,
        grid=(DEPTH, 6 * D_MODEL // tn),
        in_specs=[
            pl.BlockSpec((nb, D_MODEL), lambda l, j: (0, 0)),
            pl.BlockSpec((1, D_MODEL, tn), lambda l, j: (l, 0, j)),
            pl.BlockSpec((1, 1, tn), lambda l, j: (l, 0, j)),
        ],
        out_specs=pl.BlockSpec((1, nb, tn), lambda l, j: (l, 0, j)),
        out_shape=jax.ShapeDtypeStruct((DEPTH, nb, 6 * D_MODEL), F32),
        compiler_params=_cparams("arbitrary", "arbitrary"),
        name="ada",
    )(c_all, ada_w, ada_b.reshape(DEPTH, 1, 6 * D_MODEL))


def _token_tile(bx, tx):
    if tx >= ROW_TILE:
        return 1, ROW_TILE
    assert tx % SUBLANES == 0
    return min(bx, ROW_TILE // tx), tx


def _mod_spec(bb, chunk):
    return pl.BlockSpec((bb, 1, D_MODEL), lambda b, t: (b, 0, chunk))


def _inproj_kernel(x_ref, sh_ref, sc_ref, g_ref, w_ref, *rest, plan):
    n_prev = sum(kind == "cols_after" for _, _, kind in plan)
    prev_refs, o_refs = list(rest[:n_prev]), rest[n_prev:]
    bb, tt, _ = x_ref.shape
    h = _rms(x_ref[...], g_ref[0:1, :]) * (1.0 + sc_ref[...]) + sh_ref[...]
    y = _dot(h.reshape(bb * tt, D_MODEL).astype(BF16), w_ref[...])
    transposed = {}
    for o_ref, (c0, n, kind) in zip(o_refs, plan):
        blk = y[:, c0:c0 + n]
        if kind == "rows":
            o_ref[...] = blk.reshape(bb, tt, n)
        elif kind == "rows_q":
            o_ref[...] = (blk * SB_SCALE).astype(BF16).reshape(bb, tt, n)
        else:
            if c0 not in transposed:
                transposed[c0] = blk.T
            if kind == "cols_after":
                o_ref[0, 0] = prev_refs.pop(0)[0]
                o_ref[1, 0] = transposed[c0]
            else:
                o_ref[0] = transposed[c0].astype(o_ref.dtype)


def _in_proj(x, mod, norm_g, layer, w_bf16, w_layer, plan, prev=()):
    bx, tx, _ = x.shape
    bb, tm = _token_tile(bx, tx)
    n = w_bf16.shape[2]
    out_specs, out_shape, prev_specs = [], [], []
    for _, wd, kind in plan:
        assert bb == 1 or kind == "rows"
        if kind in ("rows", "rows_q"):
            out_specs.append(pl.BlockSpec((bb, tm, wd), lambda b, t: (b, t, 0)))
            out_shape.append(jax.ShapeDtypeStruct((bx, tx, wd), F32 if kind == "rows" else BF16))
        elif kind == "cols_after":
            prev_specs.append(pl.BlockSpec((1, wd, tm), lambda b, t: (b, 0, t)))
            out_specs.append(pl.BlockSpec((2, 1, wd, tm), lambda b, t: (0, b, 0, t)))
            out_shape.append(jax.ShapeDtypeStruct((2, bx, wd, tx), F32))
        else:
            out_specs.append(pl.BlockSpec((1, wd, tm), lambda b, t: (b, 0, t)))
            out_shape.append(jax.ShapeDtypeStruct((bx, wd, tx), F32 if kind == "cols" else BF16))
    assert len(prev_specs) == len(prev)
    return pl.pallas_call(
        functools.partial(_inproj_kernel, plan=tuple(plan)),
        grid=(bx // bb, tx // tm),
        in_specs=[
            pl.BlockSpec((bb, tm, D_MODEL), lambda b, t: (b, t, 0)),
            _mod_spec(bb, 0),
            _mod_spec(bb, 1),
            pl.BlockSpec((None, 4, D_MODEL), lambda b, t: (layer, 0, 0)),
            pl.BlockSpec((None, D_MODEL, n), lambda b, t: (w_layer, 0, 0)),
        ] + prev_specs,
        out_specs=out_specs,
        out_shape=out_shape,
        compiler_params=_cparams("arbitrary", "arbitrary"),
        name="in_proj",
    )(x, mod, mod, norm_g, w_bf16, *prev)


def _hgrn_direct(s, h, q_ref, i_ref, b_scr, kk_scr, oi_scr, o_scr, *, chunk):
    sub = min(chunk, HG_DIRECT_SUB)
    sl = slice(h * HG_D, (h + 1) * HG_D)
    rows = lax.broadcasted_iota(jnp.int32, (chunk, 1), 0)
    srow = lax.broadcasted_iota(jnp.int32, (sub, 1), 0)

    def sub_block(i, carry):
        r0 = pl.multiple_of(i * sub, sub)
        b_i = b_scr[s, pl.ds(r0, sub), sl]
        q_i = q_ref[s, pl.ds(r0, sub), sl]
        kk_i = kk_scr[s, pl.ds(r0, sub), sl]
        v_i = i_ref[s, pl.ds(r0, sub), sl]
        o = oi_scr[s, pl.ds(r0, sub), sl]
        if chunk > sub:
            anchor = b_i[0:1, :]
            qt = q_i * jnp.exp(b_i - anchor)
            kt = jnp.where(rows < r0, kk_scr[s, :, sl] * jnp.exp(jnp.minimum(anchor - b_scr[s, :, sl], 0.0)), 0.0)
            a_off = _dot_nt(qt.astype(BF16), kt.astype(BF16))
            o = o + _dot(a_off.astype(BF16), i_ref[s, :, sl].astype(BF16))
        o_diag = jnp.zeros((sub, HG_D), F32)
        for tt in range(sub):
            e = jnp.exp(jnp.minimum(b_i[tt:tt + 1, :] - b_i, 0.0))
            d = jnp.sum(q_i[tt:tt + 1, :] * e * kk_i, axis=-1, keepdims=True)
            d = jnp.where(srow <= tt, d, 0.0)
            row = jnp.sum(d * v_i, axis=0, keepdims=True)
            o_diag = jnp.where(srow == tt, row, o_diag)
        o_scr[s, pl.ds(r0, sub), sl] = o + o_diag
        return carry

    lax.fori_loop(0, chunk // sub, sub_block, 0)


def _hgrn_kernel(q_ref, f_ref, i_ref, g_ref, lb_ref, gn_ref, s0_ref, *rest,
                 nb, chunk, sub, single, stacked, safe_decay):
    if stacked:
        sprev_ref, o_ref, sout2_ref, st_scr, b_scr, kk_scr, oi_scr, o_scr = rest
        sout2_ref[0] = sprev_ref[...]
        sout_ref = sout2_ref.at[1]
    else:
        o_ref, sout_ref, st_scr, b_scr, kk_scr, oi_scr, o_scr = rest
    t = pl.program_id(1)
    if not single:
        @pl.when(t == 0)
        def _():
            for s in range(nb):
                for h in range(HG_HEADS):
                    st_scr[s * HG_HEADS + h] = s0_ref[s, h].T

    worst = [_hgrn_main(s, q_ref, f_ref, i_ref, lb_ref, s0_ref, sout_ref, st_scr, b_scr, kk_scr, oi_scr, o_scr,
                        chunk=chunk, sub=sub, single=single, safe_decay=safe_decay) for s in range(nb)]

    for s in range(nb):
        @pl.when(jnp.max(worst[s]) > safe_decay)
        def _(s=s):
            for h in range(HG_HEADS):
                _hgrn_direct(s, h, q_ref, i_ref, b_scr, kk_scr, oi_scr, o_scr, chunk=chunk)

    gn = gn_ref[...]
    for s in range(nb):
        g = g_ref[s]
        for h in range(HG_HEADS):
            sl = slice(h * HG_D, (h + 1) * HG_D)
            gh = g[:, sl]
            o_ref[s, :, sl] = _rms(o_scr[s, :, sl], gn[:, sl]) * (gh * _sigmoid(gh))

    if not single:
        @pl.when(t == pl.num_programs(1) - 1)
        def _():
            for s in range(nb):
                for h in range(HG_HEADS):
                    sout_ref[s, h] = st_scr[s * HG_HEADS + h].T


def _hgrn_main(s, q_ref, f_ref, i_ref, lb_ref, s0_ref, sout_ref, st_scr, b_scr, kk_scr, oi_scr, o_scr,
               *, chunk, sub, single, safe_decay):
    lb = lb_ref[...]
    q = q_ref[s]
    fl = f_ref[s]
    v = i_ref[s]
    e = jnp.exp(-jnp.abs(fl))
    d = 1.0 + e
    r = 1.0 / d
    c = jnp.log1p(-lb) + (jnp.minimum(fl, 0.0) - jnp.log(d))
    a = jnp.log(lb)
    log_f = jnp.maximum(a, c) + jnp.log(1.0 + jnp.exp(-jnp.abs(a - c)))
    kk = (1.0 - lb) * jnp.where(fl >= 0.0, e * r, r)
    ri = lax.broadcasted_iota(jnp.int32, (chunk, chunk), 0)
    ci = lax.broadcasted_iota(jnp.int32, (chunk, chunk), 1)
    tril = jnp.where(ri >= ci, 1.0, 0.0).astype(BF16)
    p1 = log_f.astype(BF16)
    r1 = log_f - p1.astype(F32)
    p2 = r1.astype(BF16)
    p3 = (r1 - p2.astype(F32)).astype(BF16)
    b3 = _dot(tril, jnp.concatenate([p1, p2, p3], axis=1))
    b = b3[:, 0:MIX_HALF] + b3[:, MIX_HALF:2 * MIX_HALF] + b3[:, 2 * MIX_HALF:3 * MIX_HALF]
    b_scr[s] = b
    kk_scr[s] = kk
    b_last = b[chunk - 1:chunk, :]
    qe = (q * jnp.exp(b)).astype(BF16)
    kd = kk * jnp.exp(b_last - b)
    dec = jnp.exp(b_last)
    npad = HG_D - chunk
    prow = lax.broadcasted_iota(jnp.int32, (HG_D, 1), 0)
    for h in range(HG_HEADS):
        sl = slice(h * HG_D, (h + 1) * HG_D)
        if single:
            st = s0_ref[s, h]
            oi_scr[s, :, sl] = _dot(qe[:, sl], st.astype(BF16))
            zpad = jnp.zeros((npad, HG_D), F32)
            m = jnp.concatenate([kd[:, sl], zpad], axis=0)
            m = jnp.where(prow == chunk, dec[:, sl], m)
            mt = m.T
            vpad = jnp.concatenate([v[:, sl], zpad], axis=0)
            sout_ref[s, h] = mt[:, chunk:chunk + 1] * st + _dot(mt.astype(BF16), vpad.astype(BF16))
        else:
            st = st_scr[s * HG_HEADS + h]
            oi_scr[s, :, sl] = _dot_nt(qe[:, sl], st.astype(BF16))
            st_scr[s * HG_HEADS + h] = dec[:, sl] * st + _dot(v[:, sl].T.astype(BF16), kd[:, sl].astype(BF16))

    n_sub = chunk // sub
    worst = jnp.zeros((1, MIX_HALF), F32)
    if single:
        zpad = jnp.zeros((npad, MIX_HALF), F32)
        kk_full = jnp.concatenate([kk, zpad], axis=0)
        b_full = jnp.concatenate([b, zpad], axis=0)
        v_full = jnp.concatenate([v, zpad], axis=0)
    else:
        kk_full, b_full, v_full = kk, b, v
    anchors = [b[i * sub:i * sub + 1, :] for i in range(n_sub)]
    for i in range(n_sub):
        worst = jnp.maximum(worst, anchors[i] - b[(i + 1) * sub - 1:(i + 1) * sub, :])
    anchor_rows = jnp.concatenate([jnp.broadcast_to(an, (sub, MIX_HALF)) for an in anchors], axis=0)
    qt = (q * jnp.exp(b - anchor_rows)).astype(BF16)
    kts = []
    for i, an in enumerate(anchors):
        n = kk_full.shape[0] if single else (i + 1) * sub
        kt = (kk_full[0:n] * jnp.exp(jnp.minimum(an - b_full[0:n], safe_decay))).astype(BF16)
        if n < kk_full.shape[0]:
            kt = jnp.concatenate([kt, jnp.zeros((kk_full.shape[0] - n, MIX_HALF), BF16)], axis=0)
        kts.append(kt)
    row_blk = lax.broadcasted_iota(jnp.int32, (chunk, HG_D), 0) // sub
    causal = (lax.broadcasted_iota(jnp.int32, (chunk, HG_D), 1)
              <= lax.broadcasted_iota(jnp.int32, (chunk, HG_D), 0))
    zero_q = jnp.zeros((chunk, HG_D), BF16)
    atts = []
    for h in range(HG_HEADS):
        sl = slice(h * HG_D, (h + 1) * HG_D)
        q_cat = jnp.concatenate([jnp.where(row_blk == i, qt[:, sl], zero_q) for i in range(n_sub)], axis=1)
        k_cat = jnp.concatenate([kt[:, sl] for kt in kts], axis=1)
        atts.append(jnp.where(causal, _dot_nt(q_cat, k_cat), 0.0).astype(BF16))
    bd = (lax.broadcasted_iota(jnp.int32, (HG_HEADS * HG_D, MIX_HALF), 0) // HG_D
          == lax.broadcasted_iota(jnp.int32, (HG_HEADS * HG_D, MIX_HALF), 1) // HG_D)
    v_bd = jnp.where(bd, jnp.concatenate([v_full] * HG_HEADS, axis=0), 0.0).astype(BF16)
    o_scr[s] = oi_scr[s] + _dot(jnp.concatenate(atts, axis=1), v_bd)
    return worst


def _hgrn(qfig, lb_l, gn_l, s0, s0_layer=0, s_prev=None, safe_decay=HG_SAFE_DECAY):
    bx, tx, _ = qfig.shape
    chunk = min(tx, HG_CHUNK)
    sub = min(chunk, HG_SUB)
    single = tx == chunk
    assert not single or chunk < HG_D
    nb = HG_SEQS if bx % HG_SEQS == 0 else 1

    def col(k):
        return pl.BlockSpec((nb, chunk, MIX_HALF), lambda b, t: (b, t, k))

    vec = pl.BlockSpec((1, MIX_HALF), lambda b, t: (0, 0))
    state = pl.BlockSpec((nb, HG_HEADS, HG_D, HG_D), lambda b, t: (b, 0, 0, 0))
    state_in = pl.BlockSpec((None, nb, HG_HEADS, HG_D, HG_D), lambda b, t: (s0_layer, b, 0, 0, 0))
    stacked = s_prev is not None
    if stacked:
        state_out = pl.BlockSpec((2, nb, HG_HEADS, HG_D, HG_D), lambda b, t: (0, b, 0, 0, 0))
        state_shape = jax.ShapeDtypeStruct((2, bx, HG_HEADS, HG_D, HG_D), F32)
    else:
        state_out = state
        state_shape = jax.ShapeDtypeStruct((bx, HG_HEADS, HG_D, HG_D), F32)
    return pl.pallas_call(
        functools.partial(_hgrn_kernel, nb=nb, chunk=chunk, sub=sub, single=single, stacked=stacked,
                          safe_decay=safe_decay),
        grid=(bx // nb, tx // chunk),
        in_specs=[col(0), col(1), col(2), col(3), vec, vec, state_in] + ([state] if stacked else []),
        out_specs=[pl.BlockSpec((nb, chunk, MIX_HALF), lambda b, t: (b, t, 0)), state_out],
        out_shape=[jax.ShapeDtypeStruct((bx, tx, MIX_HALF), F32), state_shape],
        scratch_shapes=([pltpu.VMEM((nb * HG_HEADS, HG_D, HG_D), F32)]
                        + [pltpu.VMEM((nb, chunk, MIX_HALF), F32)] * 4),
        compiler_params=_cparams("arbitrary", "arbitrary"),
        name="hgrn",
    )(qfig, qfig, qfig, qfig, lb_l, gn_l, s0, *([s_prev] if stacked else []))


def _sb_prompt_kernel(bias_ref, q_ref, kt_ref, vt_ref, o_ref, acc_scr, carry_scr, *, blk, per_trip):
    g = pl.program_id(1)
    qi = pl.program_id(2)
    lane = lax.broadcasted_iota(jnp.int32, (1, LANES), 1)
    ri = lax.broadcasted_iota(jnp.int32, (blk, blk), 0)
    ci = lax.broadcasted_iota(jnp.int32, (blk, blk), 1)
    upper = jnp.where(ri > ci, 1.0, 0.0).astype(BF16)
    tri = ci < ri
    causal = jnp.concatenate([tri, tri], axis=0)
    second = lax.broadcasted_iota(jnp.int32, (2 * blk, 1), 0) >= blk
    qs, bias = [], []
    for p in range(SB_PAIRS):
        q = q_ref[0, :, p * LANES:(p + 1) * LANES]
        zero = jnp.zeros_like(q)
        qs.append(jnp.concatenate([jnp.where(lane < SB_HD, q, zero), jnp.where(lane >= SB_HD, q, zero)], axis=0))
        h0 = 2 * (g * SB_PAIRS + p)
        bias.append(jnp.where(second, bias_ref[h0 + 1], bias_ref[h0]))
    acc_scr[...] = jnp.zeros_like(acc_scr)
    carry_scr[...] = jnp.zeros_like(carry_scr)

    def step(kbs, masked):
        for p in range(SB_PAIRS):
            carry = carry_scr[p]
            pv = None
            for kb in kbs:
                c0 = pl.multiple_of(kb * blk, blk)
                kblk = kt_ref[0, p * LANES:(p + 1) * LANES, pl.ds(c0, blk)]
                vblk = vt_ref[0, p * LANES:(p + 1) * LANES, pl.ds(c0, blk)]
                ls, lk = _log_sigmoid_pair(_dot(qs[p], kblk) + bias[p])
                if masked:
                    lk = jnp.where(causal, lk, 0.0)
                lkb = lk.astype(BF16)
                sfx = _dot(lkb, upper)
                w = jnp.exp(ls + sfx + jnp.concatenate([carry] * (blk // LANES), axis=1))
                if masked:
                    w = jnp.where(causal, w, 0.0)
                pv_k = _dot_nt(w.astype(BF16), vblk)
                pv = pv_k if pv is None else pv + pv_k
                carry = carry + jnp.broadcast_to(sfx[:, 0:1] + lk[:, 0:1], (2 * blk, LANES))
            acc_scr[p] += pv
            carry_scr[p] = carry

    step([qi], True)

    def body(j, c):
        kb = qi - 1 - per_trip * j
        step([kb - i for i in range(per_trip)], False)
        return c

    lax.fori_loop(0, qi // per_trip, body, 0)
    left = qi % per_trip

    def tail(j, c):
        step([left - 1 - j], False)
        return c

    lax.fori_loop(0, left, tail, 0)
    for p in range(SB_PAIRS):
        o_ref[0, :, p * LANES:(p + 1) * LANES] = jnp.where(lane < SB_HD, acc_scr[p, 0:blk], acc_scr[p, blk:2 * blk])


def _sb_prompt(bias_l, q_bf16, kt_bf16, vt_bf16):
    bx, tx, _ = q_bf16.shape
    blk = SB_BLOCK
    wd = SB_PAIRS * LANES
    rows = pl.BlockSpec((1, blk, wd), lambda b, g, qi: (b, qi, g))
    full = pl.BlockSpec((1, wd, tx), lambda b, g, qi: (b, g, 0))
    return pl.pallas_call(
        functools.partial(_sb_prompt_kernel, blk=blk, per_trip=SB_PER_TRIP),
        grid=(bx, MIX_HALF // wd, tx // blk),
        in_specs=[pl.BlockSpec(memory_space=pltpu.SMEM), rows, full, full],
        out_specs=rows,
        out_shape=jax.ShapeDtypeStruct((bx, tx, MIX_HALF), F32),
        scratch_shapes=[pltpu.VMEM((SB_PAIRS, 2 * blk, LANES), F32), pltpu.VMEM((SB_PAIRS, 2 * blk, LANES), F32)],
        compiler_params=_cparams("arbitrary", "arbitrary", "arbitrary"),
        name="sb_prompt",
    )(bias_l, q_bf16, kt_bf16, vt_bf16)


def _sb_sample_kernel(pt_ref, q_ref, kn_ref, vn_ref, bias_ref, *rest, tq, page, n_pages):
    kc_refs, vc_refs, o_ref = rest[:n_pages], rest[n_pages:2 * n_pages], rest[2 * n_pages]
    nrow = SB_HEADS * tq
    rowh = lax.broadcasted_iota(jnp.int32, (nrow, MIX_HALF), 0) // tq
    colh = lax.broadcasted_iota(jnp.int32, (nrow, MIX_HALF), 1) // SB_HD
    own = rowh == colh
    bias = bias_ref[...]
    ri = lax.broadcasted_iota(jnp.int32, (page, page), 0)
    ci = lax.broadcasted_iota(jnp.int32, (page, page), 1)
    upper = jnp.where(ri > ci, 1.0, 0.0).astype(BF16)
    qbd = (jnp.where(own, jnp.concatenate([q_ref[0]] * SB_HEADS, axis=0), 0.0) * SB_SCALE).astype(BF16)
    pad = jnp.zeros((page - tq, MIX_HALF), F32)
    kn = jnp.concatenate([kn_ref[0], pad], axis=0).astype(BF16)
    vn = jnp.concatenate([vn_ref[0], pad], axis=0).astype(BF16)
    qpos = lax.broadcasted_iota(jnp.int32, (nrow, page), 0) % tq
    kpos = lax.broadcasted_iota(jnp.int32, (nrow, page), 1)
    new_mask = kpos < qpos

    ls0, lk0 = _log_sigmoid_pair(_dot_nt(qbd, kn) + bias)
    ls_all, lk_all = [ls0], [jnp.where(new_mask, lk0, 0.0)]
    for j in range(n_pages):
        ls, lk = _log_sigmoid_pair(_dot(qbd, kc_refs[j][0, 0].astype(BF16)) + bias)
        ls_all.append(ls)
        lk_all.append(lk)
    lk_cat = jnp.concatenate(lk_all, axis=0)
    suffix = _split_dot(lk_cat, upper, 2)
    total = jnp.sum(lk_cat, axis=-1, keepdims=True)
    carry = jnp.zeros((nrow, 1), F32)
    acc = jnp.zeros((nrow, MIX_HALF), F32)
    for j in range(n_pages + 1):
        r = slice(j * nrow, (j + 1) * nrow)
        w = jnp.exp(ls_all[j] + suffix[r] + carry)
        if j == 0:
            acc = acc + _dot(jnp.where(new_mask, w, 0.0).astype(BF16), vn)
        else:
            acc = acc + _dot_nt(w.astype(BF16), vc_refs[j - 1][0, 0].astype(BF16))
        carry = carry + total[r]
    masked = jnp.where(own, acc, 0.0)
    out = masked[0:tq, :]
    for h in range(1, SB_HEADS):
        out = out + masked[h * tq:(h + 1) * tq, :]
    o_ref[0] = out


def _sb_sample(layer, page_table, bias_rows, q, k_new, v_new, cache_k, cache_v):
    bx, tq, _ = q.shape
    n_pages = page_table.shape[1]
    page = cache_k.shape[3]
    nrow = SB_HEADS * tq
    seq = pl.BlockSpec((1, tq, MIX_HALF), lambda b, pt: (b, 0, 0))

    def cache(j):
        return pl.BlockSpec((1, 1, MIX_HALF, page),
                            lambda b, pt: (layer, pt[b * n_pages + (n_pages - 1 - j)], 0, 0))

    pages = [cache(j) for j in range(n_pages)]
    return pl.pallas_call(
        functools.partial(_sb_sample_kernel, tq=tq, page=page, n_pages=n_pages),
        grid_spec=pltpu.PrefetchScalarGridSpec(
            num_scalar_prefetch=1,
            grid=(bx,),
            in_specs=[seq, seq, seq, pl.BlockSpec((nrow, 1), lambda b, pt: (0, 0))] + pages + pages,
            out_specs=seq,
        ),
        out_shape=jax.ShapeDtypeStruct((bx, tq, MIX_HALF), F32),
        compiler_params=_cparams("arbitrary"),
        name="sb_sample",
    )(page_table.reshape(-1), q, k_new, v_new, bias_rows, *([cache_k] * n_pages), *([cache_v] * n_pages))


def _odd_kernel(*refs, nb, tm, q_start):
    for s in range(nb):
        _odd_one(s, *refs, tm=tm, q_start=q_start)


def _odd_one(s, u_ref, a_ref, gt_ref, hu_ref, hg_ref, pw_ref, ps_ref, cw_ref, cb_ref, lg_ref, lbias_ref,
             po_ref, co_ref, nu_ref, ng_ref, eu_scr, eg_scr, sh_scr, *, tm, q_start):
    t = pl.program_id(1)
    hu0 = 2 * SUBLANES
    hg0 = 4 * SUBLANES

    @pl.when(t == 0)
    def _():
        eu_scr[0:hu0 - POOL_HIST, :] = jnp.zeros((hu0 - POOL_HIST, MIX_HALF), F32)
        eg_scr[0:hg0 - CONV_HIST, :] = jnp.zeros((hg0 - CONV_HIST, MIX_HALF), F32)
        eu_scr[hu0 - POOL_HIST:hu0, :] = hu_ref[s]
        eg_scr[hg0 - CONV_HIST:hg0, :] = hg_ref[s]

    @pl.when(t > 0)
    def _():
        eu_scr[0:hu0, :] = eu_scr[tm:tm + hu0, :]
        eg_scr[0:hg0, :] = eg_scr[tm:tm + hg0, :]

    u = u_ref[s]
    gt = gt_ref[s]
    glu = a_ref[s] * _sigmoid(gt)
    eu_scr[hu0:hu0 + tm, :] = u
    eg_scr[hg0:hg0 + tm, :] = glu

    pos = q_start + t * tm + lax.broadcasted_iota(jnp.int32, (tm, 1), 0)
    for gi, w in enumerate(POOL_WINDOWS):
        c0 = gi * POOL_GC
        wsum = u[:, c0:c0 + POOL_GC]
        for d in range(1, w):
            wsum = wsum + eu_scr[hu0 - d:hu0 - d + tm, c0:c0 + POOL_GC]
        cnt = jnp.minimum(w, pos + 1).astype(F32)
        pooled = wsum / cnt - u[:, c0:c0 + POOL_GC]
        po = _dot(pooled.astype(BF16), pw_ref[gi].astype(BF16))
        po_ref[s, :, c0:c0 + POOL_GC] = po * ps_ref[:, c0:c0 + POOL_GC]

    cv = jnp.broadcast_to(cb_ref[...], (tm, MIX_HALF))
    first = hg0 - CONV_HIST
    for r in range(SUBLANES):
        taps = [j for j in range(CONV_WIDTH) if (first + j) % SUBLANES == r]
        if not taps:
            continue
        span = (first + taps[-1]) // SUBLANES * SUBLANES
        if r == 0:
            src = eg_scr
        else:
            sh_scr[0:span + tm, :] = eg_scr[r:r + span + tm, :]
            src = sh_scr
        for j in taps:
            a0 = first + j - r
            cv = cv + cw_ref[j:j + 1, :] * src[a0:a0 + tm, :]
    mu = jnp.mean(cv, axis=-1, keepdims=True)
    dv = cv - mu
    var = jnp.mean(dv * dv, axis=-1, keepdims=True)
    y = dv * lax.rsqrt(var + EPS) * lg_ref[...] + lbias_ref[...]
    co_ref[s] = y * _sigmoid(y)

    @pl.when(t == pl.num_programs(1) - 1)
    def _():
        nu_ref[s] = eu_scr[hu0 + tm - POOL_HIST:hu0 + tm, :]
        ng_ref[s] = eg_scr[hg0 + tm - CONV_HIST:hg0 + tm, :]


def _odd(u, a, gt, hist_u, hist_g, io, pool_w, pool_scale, conv_w, conv_b, ln_g, ln_b, q_start):
    bx, tx, _ = u.shape
    tm = min(tx, 256)
    nb = ODD_SEQS if (tx == tm and bx % ODD_SEQS == 0) else 1
    tile = pl.BlockSpec((nb, tm, MIX_HALF), lambda b, t: (b, t, 0))
    vec = pl.BlockSpec((1, MIX_HALF), lambda b, t: (0, 0))
    hu_spec = pl.BlockSpec((None, nb, POOL_HIST, MIX_HALF), lambda b, t: (io if hist_u.shape[0] > 1 else 0, b, 0, 0))
    hg_spec = pl.BlockSpec((None, nb, CONV_HIST, MIX_HALF), lambda b, t: (io if hist_g.shape[0] > 1 else 0, b, 0, 0))
    return pl.pallas_call(
        functools.partial(_odd_kernel, nb=nb, tm=tm, q_start=q_start),
        grid=(bx // nb, tx // tm),
        in_specs=[tile, tile, tile, hu_spec, hg_spec,
                  pl.BlockSpec((None, len(POOL_WINDOWS), POOL_GC, POOL_GC), lambda b, t: (io, 0, 0, 0)),
                  vec,
                  pl.BlockSpec((None, CONV_WIDTH, MIX_HALF), lambda b, t: (io, 0, 0)),
                  vec, vec, vec],
        out_specs=[tile, tile,
                   pl.BlockSpec((nb, POOL_HIST, MIX_HALF), lambda b, t: (b, 0, 0)),
                   pl.BlockSpec((nb, CONV_HIST, MIX_HALF), lambda b, t: (b, 0, 0))],
        out_shape=[jax.ShapeDtypeStruct((bx, tx, MIX_HALF), F32), jax.ShapeDtypeStruct((bx, tx, MIX_HALF), F32),
                   jax.ShapeDtypeStruct((bx, POOL_HIST, MIX_HALF), F32),
                   jax.ShapeDtypeStruct((bx, CONV_HIST, MIX_HALF), F32)],
        scratch_shapes=[pltpu.VMEM((2 * SUBLANES + tm, MIX_HALF), F32),
                        pltpu.VMEM((4 * SUBLANES + tm, MIX_HALF), F32),
                        pltpu.VMEM((4 * SUBLANES + tm, MIX_HALF), F32)],
        compiler_params=_cparams("arbitrary", "arbitrary"),
        name="odd_mixer",
    )(u, a, gt, hist_u, hist_g, pool_w, pool_scale[io:io + 1], conv_w, conv_b[io:io + 1], ln_g[io:io + 1],
      ln_b[io:io + 1])


def _post_kernel(a_ref, b_ref, x_ref, g1_ref, sh2_ref, sc2_ref, g2_ref, ng_ref, woa_ref, wob_ref,
                 w1_ref, w2_ref, o_ref):
    bb, tt, _ = x_ref.shape
    rows = bb * tt
    m = (_dot(a_ref[...].reshape(rows, MIX_HALF).astype(BF16), woa_ref[...])
         + _dot(b_ref[...].reshape(rows, MIX_HALF).astype(BF16), wob_ref[...]))
    x1 = x_ref[...] + g1_ref[...] * _rms(m, ng_ref[1:2, :]).reshape(bb, tt, D_MODEL)
    h2 = (_rms(x1, ng_ref[2:3, :]) * (1.0 + sc2_ref[...]) + sh2_ref[...]).reshape(rows, D_MODEL).astype(BF16)
    acc = None
    for c0 in range(0, D_FF, FF_TILE):
        hc = jnp.maximum(_dot(h2, w1_ref[:, c0:c0 + FF_TILE]), 0.0)
        d = _dot((hc * hc).astype(BF16), w2_ref[c0:c0 + FF_TILE, :])
        acc = d if acc is None else acc + d
    o_ref[...] = x1 + g2_ref[...] * _rms(acc, ng_ref[3:4, :]).reshape(bb, tt, D_MODEL)


def _post(a, b, x, mod, norm_g, layer, wo_bf16, wo_layer, w1_bf16, w2_bf16):
    bx, tx, _ = x.shape
    bb, tm = _token_tile(bx, tx)
    half = pl.BlockSpec((bb, tm, MIX_HALF), lambda b_, t: (b_, t, 0))
    full = pl.BlockSpec((bb, tm, D_MODEL), lambda b_, t: (b_, t, 0))
    once = pl.Buffered(1)
    return pl.pallas_call(
        _post_kernel,
        grid=(bx // bb, tx // tm),
        in_specs=[half, half, full,
                  _mod_spec(bb, 2), _mod_spec(bb, 3), _mod_spec(bb, 4), _mod_spec(bb, 5),
                  pl.BlockSpec((None, 4, D_MODEL), lambda b_, t: (layer, 0, 0)),
                  pl.BlockSpec((None, MIX_HALF, D_MODEL), lambda b_, t: (wo_layer, 0, 0), pipeline_mode=once),
                  pl.BlockSpec((None, MIX_HALF, D_MODEL), lambda b_, t: (wo_layer, 1, 0), pipeline_mode=once),
                  pl.BlockSpec((None, D_MODEL, D_FF), lambda b_, t: (layer, 0, 0), pipeline_mode=once),
                  pl.BlockSpec((None, D_FF, D_MODEL), lambda b_, t: (layer, 0, 0), pipeline_mode=once)],
        out_specs=full,
        out_shape=jax.ShapeDtypeStruct((bx, tx, D_MODEL), F32),
        compiler_params=_cparams("arbitrary", "arbitrary"),
        name="post",
    )(a, b, x, mod, mod, mod, mod, norm_g, wo_bf16, wo_bf16, w1_bf16, w2_bf16)


def kernel(x_prompt, x_sample, c_prompt, c_sample, cache_sb_k, cache_sb_v, state_hgrn, state_pool, state_conv,
           page_table, norm_g, ada_w, ada_b, even_w_in, even_w_out, sb_bias, hgrn_lb_logits, hgrn_norm_g,
           odd_w_in, odd_w_out, pool_w, pool_scale, conv_w, conv_b, conv_ln_g, conv_ln_b, mlp_w1, mlp_w2):
    n_even = even_w_in.shape[0]
    assert n_even == 2 and DEPTH == 4
    bp, tp, _ = x_prompt.shape
    bs, ts, _ = x_sample.shape
    n_pool, page = cache_sb_k.shape[1], cache_sb_k.shape[2]
    past_len = page_table.shape[1] * page
    half = MIX_HALF

    lb_cum = jnp.cumsum(jax.nn.softmax(hgrn_lb_logits.astype(F32), axis=0), axis=0)
    lower_bounds = jnp.maximum(lb_cum - lb_cum[:1], 0.0)

    mod_all = _ada(jnp.concatenate([c_prompt, c_sample], axis=0), ada_w, ada_b)
    even_w_in_b = even_w_in.astype(BF16)
    even_w_out_b = even_w_out.astype(BF16)
    odd_w_in_b = odd_w_in.astype(BF16)
    odd_w_out_b = odd_w_out.astype(BF16)
    w1_b = mlp_w1.astype(BF16)
    w2_b = mlp_w2.astype(BF16)
    cache_k = cache_sb_k.transpose(0, 1, 3, 4, 2).reshape(n_even, n_pool, half, page)
    cache_v = cache_sb_v.transpose(0, 1, 3, 4, 2).reshape(n_even, n_pool, half, page)
    gn = hgrn_norm_g.reshape(n_even, half)

    def run(x, mods, is_sample):
        bx, tx, _ = x.shape
        s_list, k_list, v_list, p_list, c_list = [], [], [], [], []
        for l in range(DEPTH):
            mod = mods[l]
            if l % 2 == 0:
                ie = l // 2
                if is_sample:
                    qfig, q_b, k_b, v_b = _in_proj(
                        x, mod, norm_g, l, even_w_in_b, ie,
                        [(0, 4 * half, "rows"), (4 * half, half, "rows"), (5 * half, half, "rows"),
                         (6 * half, half, "rows")])
                    o_a, s_new = _hgrn(qfig.reshape(bs, ts, 4 * half), lower_bounds[ie:ie + 1], gn[ie:ie + 1],
                                       state_hgrn, ie, s_prev=s_list[0] if ie == 1 else None)
                    bias_rows = jnp.repeat(sb_bias[ie].astype(F32), ts).reshape(SB_HEADS * ts, 1)
                    k_b = k_b.reshape(bs, ts, half)
                    v_b = v_b.reshape(bs, ts, half)
                    o_b = _sb_sample(ie, page_table, bias_rows, q_b.reshape(bs, ts, half), k_b, v_b, cache_k, cache_v)
                    k_list.append(k_b.reshape(bs, ts, SB_HEADS, SB_HD))
                    v_list.append(v_b.reshape(bs, ts, SB_HEADS, SB_HD))
                else:
                    kv_kind = "cols" if ie == 0 else "cols_after"
                    qfig, q_b, k_t, k_tb, v_t, v_tb = _in_proj(
                        x, mod, norm_g, l, even_w_in_b, ie,
                        [(0, 4 * half, "rows"), (4 * half, half, "rows_q"), (5 * half, half, kv_kind),
                         (5 * half, half, "cols_bf16"), (6 * half, half, kv_kind), (6 * half, half, "cols_bf16")],
                        prev=() if ie == 0 else (k_list[0], v_list[0]))
                    o_a, s_new = _hgrn(qfig, lower_bounds[ie:ie + 1], gn[ie:ie + 1],
                                       jnp.zeros((1, bx, HG_HEADS, HG_D, HG_D), F32))
                    o_b = _sb_prompt(sb_bias[ie].astype(F32), q_b, k_tb, v_tb)
                    k_list.append(k_t)
                    v_list.append(v_t)
                mix_a, mix_b, w_out, wo_layer = o_a.reshape(bx, tx, half), o_b.reshape(bx, tx, half), even_w_out_b, ie
                s_list.append(s_new)
            else:
                io = l // 2
                u, a, gt = _in_proj(x, mod, norm_g, l, odd_w_in_b, io,
                                    [(0, half, "rows"), (half, half, "rows"), (2 * half, half, "rows")])
                if is_sample:
                    hb, ht = bs, ts
                    hist_u, hist_g, q_start = state_pool, state_conv, past_len
                else:
                    hb, ht = bx, tx
                    hist_u = jnp.zeros((1, bx, POOL_HIST, half), F32)
                    hist_g = jnp.zeros((1, bx, CONV_HIST, half), F32)
                    q_start = 0
                po, co, new_u, new_g = _odd(u.reshape(hb, ht, half), a.reshape(hb, ht, half),
                                            gt.reshape(hb, ht, half), hist_u, hist_g, io, pool_w, pool_scale,
                                            conv_w, conv_b, conv_ln_g, conv_ln_b, q_start)
                mix_a, mix_b, w_out, wo_layer = po.reshape(bx, tx, half), co.reshape(bx, tx, half), odd_w_out_b, io
                p_list.append(new_u)
                c_list.append(new_g)
            x = _post(mix_a, mix_b, x, mod, norm_g, l, w_out, wo_layer, w1_b, w2_b)
        if is_sample:
            s_all, k_all, v_all = s_list[1], jnp.stack(k_list), jnp.stack(v_list)
        else:
            s_all = jnp.stack(s_list)
            k_all = k_list[1].reshape(2, bx, SB_HEADS, SB_HD, tx).transpose(0, 1, 4, 2, 3)
            v_all = v_list[1].reshape(2, bx, SB_HEADS, SB_HD, tx).transpose(0, 1, 4, 2, 3)
        return (x, s_all, k_all, v_all, jnp.stack(p_list), jnp.stack(c_list))

    mods_p = [mod_all[l, :bp].reshape(bp, 1, 6 * D_MODEL) for l in range(DEPTH)]
    mods_s = [mod_all[l, bp:].reshape(bs, 1, 6 * D_MODEL) for l in range(DEPTH)]
    y_p, s_p, k_p, v_p, pool_p, conv_p = run(x_prompt, mods_p, False)
    y_s, s_s, k_s, v_s, pool_s, conv_s = run(x_sample, mods_s, True)
    return (y_p, y_s, k_p, v_p, k_s, v_s, s_p, s_s, pool_p, pool_s, conv_p, conv_s)
```
